```python
import math
import jax, jax.numpy as jnp
from jax import lax
import numpy as np

D_MODEL = 1024
BATCH = 8
SEQ = 4096
DEPTH = 4

EPS = 1e-6
GLA_HEADS = 4
GLA_DK = D_MODEL // 2 // GLA_HEADS
GLA_DV = D_MODEL // GLA_HEADS
GLA_KEY = GLA_HEADS * GLA_DK
GLA_VAL = GLA_HEADS * GLA_DV
GLA_GATE_RANK = 16
GLA_GATE_NORM = 16.0
GLA_CHUNK = 64
SSM_EXPAND = 2
SSM_INNER = SSM_EXPAND * D_MODEL
SSM_HEADDIM = 64
SSM_HEADS = SSM_INNER // SSM_HEADDIM
SSM_GROUPS = 4
SSM_STATE = 128
SSM_CONV = 4
SSM_CHUNK = 128
SSM_XBC = SSM_INNER + 2 * SSM_GROUPS * SSM_STATE
SSM_DT_MIN = 0.001
SSM_DT_MAX = 0.1
FFN_HIDDEN = ((8 * D_MODEL // 3 + 255) // 256) * 256
IN_SPLITS = (GLA_KEY, GLA_KEY, GLA_VAL, GLA_VAL, GLA_GATE_RANK, SSM_INNER, SSM_XBC, SSM_HEADS, D_MODEL, D_MODEL)
IN_DIM = sum(IN_SPLITS)

kernel_name = "hybrid_gla_ssd_gated_merge"


def split_cols(u, sizes):
    idx, acc = [], 0
    for s in sizes[:-1]:
        acc += s
        idx.append(acc)
    return jnp.split(u, idx, axis=-1)


def rms_norm(x, w):
    xf = x.astype(jnp.float32)
    y = xf * lax.rsqrt(jnp.mean(xf * xf, axis=-1, keepdims=True) + EPS)
    return (y * w.astype(jnp.float32)).astype(x.dtype)


def gla_chunked(q, k, v, g):
    Bsz, T, H, dk = q.shape
    dv = v.shape[-1]
    n = T // GLA_CHUNK

    def to_chunks(a):
        return a.reshape(Bsz, n, GLA_CHUNK, H, a.shape[-1]).transpose(1, 0, 3, 2, 4)

    qc, kc, vc, gc = to_chunks(q * (GLA_DK ** -0.5)), to_chunks(k), to_chunks(v), to_chunks(g)
    causal = jnp.tril(jnp.ones((GLA_CHUNK, GLA_CHUNK), dtype=bool))

    def step(S, inp):
        qi, ki, vi, gi = inp
        b = jnp.cumsum(gi, axis=2)
        b_last = b[:, :, -1:, :]
        o_inter = jnp.einsum('bhcd,bhde->bhce', qi * jnp.exp(b), S)
        diff = b[:, :, :, None, :] - b[:, :, None, :, :]
        decay = jnp.exp(jnp.where(causal[:, :, None], diff, -jnp.inf))
        A = jnp.einsum('bhid,bhjd,bhijd->bhij', qi, ki, decay)
        o = o_inter + jnp.einsum('bhij,bhje->bhie', A, vi)
        S = jnp.exp(b_last)[:, :, 0, :, None] * S + jnp.einsum('bhcd,bhce->bhde', ki * jnp.exp(b_last - b), vi)
        return S, o

    S0 = jnp.zeros((Bsz, H, dk, dv), jnp.float32)
    _, o = lax.scan(step, S0, (qc, kc, vc, gc))
    return o.transpose(1, 0, 3, 2, 4).reshape(Bsz, T, H, dv)


def ssd_chunked(x, dt, A, Bm, Cm):
    Bsz, T, H, P = x.shape
    G, N = Bm.shape[2], Bm.shape[3]
    E = H // G
    n = T // SSM_CHUNK
    L = SSM_CHUNK
    xc = x.reshape(Bsz, n, L, G, E, P).transpose(1, 0, 2, 3, 4, 5)
    ac = (dt * A).reshape(Bsz, n, L, G, E).transpose(1, 0, 3, 4, 2)
    dtc = dt.reshape(Bsz, n, L, G, E).transpose(1, 0, 2, 3, 4)
    Bc = Bm.reshape(Bsz, n, L, G, N).transpose(1, 0, 2, 3, 4)
    Cc = Cm.reshape(Bsz, n, L, G, N).transpose(1, 0, 2, 3, 4)
    causal = jnp.tril(jnp.ones((L, L), dtype=bool))

    def step(S, inp):
        xi, ai, dti, Bi, Ci = inp
        cum = jnp.cumsum(ai, axis=-1)
        seg = cum[..., :, None] - cum[..., None, :]
        Lmat = jnp.exp(jnp.where(causal, seg, -jnp.inf))
        CB = jnp.einsum('blgn,bsgn->bgls', Ci, Bi)
        xdt = xi * dti[..., None]
        y_diag = jnp.einsum('bgls,bgels,bsgep->blgep', CB, Lmat, xdt)
        y_off = jnp.einsum('blgn,bgepn,bgel->blgep', Ci, S, jnp.exp(cum))
        decay_state = jnp.exp(cum[..., -1:] - cum)
        S = jnp.exp(cum[..., -1])[..., None, None] * S + jnp.einsum('bsgn,bges,bsgep->bgepn', Bi, decay_state, xdt)
        return S, y_diag + y_off

    S0 = jnp.zeros((Bsz, G, E, P, N), jnp.float32)
    _, y = lax.scan(step, S0, (xc, ac, dtc, Bc, Cc))
    return y.transpose(1, 0, 2, 3, 4, 5).reshape(Bsz, T, H, P)


def causal_depthwise_conv(u, w, b):
    C = u.shape[-1]
    y = lax.conv_general_dilated(u, w[:, None, :].astype(u.dtype), window_strides=(1,),
                                 padding=((SSM_CONV - 1, 0),), dimension_numbers=('NWC', 'WIO', 'NWC'),
                                 feature_group_count=C)
    return y + b.astype(u.dtype)


def hybrid_mixer(h, w_in, gla_w2, gla_b, gla_norm_w, conv_w, conv_b, dt_bias, A_log, D_skip,
                 ssm_norm_w, w_ya, w_yb, w_out):
    Bsz, T, _ = h.shape
    f32 = lambda t: t.astype(jnp.float32)
    u = h @ w_in
    q, k, v, r, glr, z, xbc, dt_raw, ga, gb = split_cols(u, IN_SPLITS)

    g = jax.nn.log_sigmoid(f32(glr @ gla_w2 + gla_b)) / GLA_GATE_NORM
    hk = lambda t, d: f32(t).reshape(Bsz, T, GLA_HEADS, d)
    o = gla_chunked(hk(q, GLA_DK), hk(k, GLA_DK), hk(v, GLA_DV), g.reshape(Bsz, T, GLA_HEADS, GLA_DK))
    o = rms_norm(o, gla_norm_w).reshape(Bsz, T, GLA_VAL)
    y_a = (o * jax.nn.silu(f32(r))).astype(h.dtype) @ w_ya

    xbc = jax.nn.silu(causal_depthwise_conv(xbc, conv_w, conv_b))
    xs, Bm, Cm = split_cols(xbc, (SSM_INNER, SSM_GROUPS * SSM_STATE, SSM_GROUPS * SSM_STATE))
    dt = jax.nn.softplus(f32(dt_raw) + f32(dt_bias))
    A = -jnp.exp(f32(A_log))
    xh = f32(xs).reshape(Bsz, T, SSM_HEADS, SSM_HEADDIM)
    y = ssd_chunked(xh, dt, A, f32(Bm).reshape(Bsz, T, SSM_GROUPS, SSM_STATE),
                    f32(Cm).reshape(Bsz, T, SSM_GROUPS, SSM_STATE))
    y = y + xh * f32(D_skip)[None, None, :, None]
    y = y.reshape(Bsz, T, SSM_INNER) * jax.nn.silu(f32(z))
    gsz = SSM_INNER // SSM_GROUPS
    y = rms_norm(y.reshape(Bsz, T, SSM_GROUPS, gsz), ssm_norm_w.reshape(SSM_GROUPS, gsz)).reshape(Bsz, T, SSM_INNER)
    y_b = y.astype(h.dtype) @ w_yb

    m = jax.nn.sigmoid(ga) * y_a + jax.nn.sigmoid(gb) * y_b
    return m @ w_out


def swiglu(h, w_in, w_out):
    gate, up = split_cols(h @ w_in, (FFN_HIDDEN, FFN_HIDDEN))
    return (jax.nn.silu(gate) * up) @ w_out


def setup_inputs(seed: int = 0) -> dict:
    key = jax.random.key(seed)
    ks = jax.random.split(key, 24)
    nrm = lambda k, shape, s: jax.random.normal(k, shape, jnp.float32) * s
    Lh = (DEPTH,)
    u_dt = jax.random.uniform(ks[9], Lh + (SSM_HEADS,), jnp.float32)
    dt0 = jnp.exp(u_dt * (math.log(SSM_DT_MAX) - math.log(SSM_DT_MIN)) + math.log(SSM_DT_MIN))
    return {
        "x": nrm(ks[0], (BATCH, SEQ, D_MODEL), 1.0),
        "norm1_w": 1.0 + nrm(ks[1], Lh + (D_MODEL,), 0.01),
        "w_in": nrm(ks[2], Lh + (D_MODEL, IN_DIM), D_MODEL ** -0.5),
        "gla_gate_w2": nrm(ks[3], Lh + (GLA_GATE_RANK, GLA_KEY), GLA_GATE_RANK ** -0.5),
        "gla_gate_b": nrm(ks[4], Lh + (GLA_KEY,), 0.1),
        "gla_norm_w": 1.0 + nrm(ks[5], Lh + (GLA_DV,), 0.01),
        "ssm_conv_w": nrm(ks[6], Lh + (SSM_CONV, SSM_XBC), SSM_CONV ** -0.5),
        "ssm_conv_b": nrm(ks[7], Lh + (SSM_XBC,), 0.01),
        "ssm_dt_bias": dt0 + jnp.log(-jnp.expm1(-dt0)),
        "ssm_A_log": jnp.log(jax.random.uniform(ks[10], Lh + (SSM_HEADS,), jnp.float32, 1.0, 16.0)),
        "ssm_D": 1.0 + nrm(ks[11], Lh + (SSM_HEADS,), 0.1),
        "ssm_norm_w": 1.0 + nrm(ks[12], Lh + (SSM_INNER,), 0.01),
        "w_branch_a": nrm(ks[13], Lh + (GLA_VAL, D_MODEL), GLA_VAL ** -0.5),
        "w_branch_b": nrm(ks[14], Lh + (SSM_INNER, D_MODEL), SSM_INNER ** -0.5),
        "w_mix_out": nrm(ks[15], Lh + (D_MODEL, D_MODEL), D_MODEL ** -0.5),
        "norm2_w": 1.0 + nrm(ks[16], Lh + (D_MODEL,), 0.01),
        "w_ffn_in": nrm(ks[17], Lh + (D_MODEL, 2 * FFN_HIDDEN), D_MODEL ** -0.5),
        "w_ffn_out": nrm(ks[18], Lh + (FFN_HIDDEN, D_MODEL), FFN_HIDDEN ** -0.5),
        "final_norm_w": 1.0 + nrm(ks[19], (D_MODEL,), 0.01),
    }


def reference(x, norm1_w, w_in, gla_gate_w2, gla_gate_b, gla_norm_w, ssm_conv_w, ssm_conv_b,
              ssm_dt_bias, ssm_A_log, ssm_D, ssm_norm_w, w_branch_a, w_branch_b, w_mix_out,
              norm2_w, w_ffn_in, w_ffn_out, final_norm_w):
    for l in range(DEPTH):
        h = rms_norm(x, norm1_w[l])
        x = x + hybrid_mixer(h, w_in[l], gla_gate_w2[l], gla_gate_b[l], gla_norm_w[l], ssm_conv_w[l],
                             ssm_conv_b[l], ssm_dt_bias[l], ssm_A_log[l], ssm_D[l], ssm_norm_w[l],
                             w_branch_a[l], w_branch_b[l], w_mix_out[l])
        x = x + swiglu(rms_norm(x, norm2_w[l]), w_ffn_in[l], w_ffn_out[l])
    return rms_norm(x, final_norm_w)
```

```python
import functools

import numpy as np
import jax
import jax.numpy as jnp
from jax import lax
from jax.experimental import pallas as pl
from jax.experimental.pallas import tpu as pltpu

F32 = jnp.float32
BF16 = jnp.bfloat16

D_MODEL = 1024
EPS = 1e-6
GLA_HEADS = 4
GLA_DK = 128
GLA_DV = 256
GLA_KEY = GLA_HEADS * GLA_DK
GLA_VAL = GLA_HEADS * GLA_DV
GLA_RANK = 16
GLA_TAU = 16.0
GLA_CHUNK = 64
GLA_LEVELS = 6
GLA_SECTIONS = 2 + GLA_LEVELS
SSM_INNER = 2048
SSM_P = 64
SSM_HEADS = 32
SSM_GROUPS = 4
SSM_N = 128
SSM_CONV = 4
SSM_CHUNK = 128
SSM_HPG = SSM_HEADS // SSM_GROUPS
SSM_GW = SSM_HPG * SSM_P
SSM_XBC = SSM_INNER + 2 * SSM_GROUPS * SSM_N
FFN_HIDDEN = 2816
IN_SPLITS = (GLA_KEY, GLA_KEY, GLA_VAL, GLA_VAL, GLA_RANK, SSM_INNER, SSM_XBC,
             SSM_HEADS, D_MODEL, D_MODEL)

OFF_Q = 0
OFF_K = OFF_Q + GLA_KEY
OFF_V = OFF_K + GLA_KEY
OFF_R = OFF_V + GLA_VAL
OFF_Z = OFF_R + GLA_VAL
OFF_X = OFF_Z + SSM_INNER
OFF_B = OFF_X + SSM_INNER
OFF_C = OFF_B + SSM_GROUPS * SSM_N
OFF_GA = OFF_C + SSM_GROUPS * SSM_N
OFF_GB = OFF_GA + D_MODEL
U_COLS = OFF_GB + D_MODEL
SMALL_COLS = 128
DT_LANE0 = GLA_RANK
TAIL_ROWS = 8

VMEM_LIMIT = 48 * 1024 * 1024


def _sigmoid(x):
    return 0.5 * jnp.tanh(0.5 * x) + 0.5


def _silu(x):
    return x * _sigmoid(x)


def _softplus(x):
    return jnp.maximum(x, 0.0) + jnp.log1p(jnp.exp(-jnp.abs(x)))


def _log_sigmoid(x):
    return jnp.minimum(x, 0.0) - jnp.log1p(jnp.exp(-jnp.abs(x)))


def _split_hi_lo(x):
    hi = x.astype(BF16)
    lo = (x - hi.astype(F32)).astype(BF16)
    return hi, lo


def _dot(a, b):
    return jnp.dot(a, b, preferred_element_type=F32)


def _dot_nt(a, b):
    return lax.dot_general(a, b, (((1,), (1,)), ((), ())), preferred_element_type=F32)


def _dot_tn(a, b):
    return lax.dot_general(a, b, (((0,), (0,)), ((), ())), preferred_element_type=F32)


def _params(semantics):
    return pltpu.CompilerParams(dimension_semantics=semantics, vmem_limit_bytes=VMEM_LIMIT)


def _rms_rows(x, w):
    return x * lax.rsqrt(jnp.mean(x * x, axis=-1, keepdims=True) + EPS) * w


def _norm_kernel(x_ref, w_ref, h_ref):
    h_ref[...] = _rms_rows(x_ref[...], w_ref[...]).astype(BF16)


def _norm(x, w, tm):
    m = x.shape[0]
    return pl.pallas_call(
        _norm_kernel,
        grid=(m // tm,),
        in_specs=[pl.BlockSpec((tm, D_MODEL), lambda i: (i, 0)),
                  pl.BlockSpec((1, D_MODEL), lambda i: (0, 0))],
        out_specs=pl.BlockSpec((tm, D_MODEL), lambda i: (i, 0)),
        out_shape=jax.ShapeDtypeStruct((m, D_MODEL), BF16),
        compiler_params=_params(("parallel",)),
        name="rms_norm",
    )(x, w)


def _inproj_kernel(h_ref, w_ref, ws_ref, u_ref, s_ref):
    h = h_ref[...]
    u_ref[...] = _dot(h, w_ref[...]).astype(BF16)

    @pl.when(pl.program_id(1) == 0)
    def _():
        s_ref[...] = _dot(h, ws_ref[...])


def _inproj(h, w_main, w_small, tm, tn):
    m = h.shape[0]
    return pl.pallas_call(
        _inproj_kernel,
        grid=(m // tm, U_COLS // tn),
        in_specs=[pl.BlockSpec((tm, D_MODEL), lambda i, j: (i, 0)),
                  pl.BlockSpec((D_MODEL, tn), lambda i, j: (0, j)),
                  pl.BlockSpec((D_MODEL, SMALL_COLS), lambda i, j: (0, 0))],
        out_specs=[pl.BlockSpec((tm, tn), lambda i, j: (i, j)),
                   pl.BlockSpec((tm, SMALL_COLS), lambda i, j: (i, 0))],
        out_shape=[jax.ShapeDtypeStruct((m, U_COLS), BF16),
                   jax.ShapeDtypeStruct((m, SMALL_COLS), F32)],
        compiler_params=_params(("parallel", "arbitrary")),
        name="in_proj",
    )(h, w_main, w_small)


def _gla_constants():
    c = GLA_CHUNK
    mat = np.zeros((GLA_SECTIONS, c, c), np.float32)
    t = np.arange(c)[:, None]
    tau = np.arange(c)[None, :]
    mat[0] = tau <= t
    mat[1] = tau > t
    for lvl in range(GLA_LEVELS):
        s = 1 << lvl
        mid = (t // (2 * s)) * (2 * s) + s
        upper = t >= mid
        mat[2 + lvl] = np.where(upper, (tau >= mid) & (tau <= t), (tau > t) & (tau <= mid - 1))
    return jnp.asarray(mat.reshape(GLA_SECTIONS * c, c), BF16)


def _gla_level_mask(lvl):
    c = GLA_CHUNK
    s = 1 << lvl
    i = lax.broadcasted_iota(jnp.int32, (c, c), 0)
    j = lax.broadcasted_iota(jnp.int32, (c, c), 1)
    same = (i // (2 * s)) == (j // (2 * s))
    return same & ((i % (2 * s)) >= s) & ((j % (2 * s)) < s)


def _gla_kernel(q_ref, k_ref, v_ref, r_ref, s_ref, w2_ref, b2_ref, nw_ref, mat_ref,
                o_ref, st_ref, *, n_chunks):
    c = GLA_CHUNK

    @pl.when(pl.program_id(2) == 0)
    def _():
        st_ref[...] = jnp.zeros_like(st_ref)

    w2 = w2_ref[...]
    b2 = b2_ref[...]
    nw = nw_ref[...]
    mat = mat_ref[...]
    ii = lax.broadcasted_iota(jnp.int32, (c, c), 0)
    jj = lax.broadcasted_iota(jnp.int32, (c, c), 1)
    masks = [_gla_level_mask(lvl) for lvl in range(GLA_LEVELS)]
    eye = ii == jj

    def chunk(ci, carry):
        r0 = pl.multiple_of(ci * c, c)
        rows = pl.ds(r0, c)
        q = q_ref[rows, :].astype(F32) * (GLA_DK ** -0.5)
        k = k_ref[rows, :].astype(F32)
        v = v_ref[rows, :]
        glr = s_ref[rows, :][:, :GLA_RANK]
        logit = jnp.dot(glr, w2, preferred_element_type=F32,
                        precision=lax.Precision.HIGHEST) + b2
        g = _log_sigmoid(logit) * (1.0 / GLA_TAU)
        g_hi, g_lo = _split_hi_lo(g)
        e2 = _dot(mat, jnp.concatenate([g_hi, g_lo], axis=1))
        ex = jnp.exp(e2[:, :GLA_DK] + e2[:, GLA_DK:])

        st = st_ref[...]
        qi = (q * ex[0:c]).astype(BF16)
        o = _dot_nt(qi, st.astype(BF16))

        a = jnp.where(eye, _dot_nt(q.astype(BF16), k.astype(BF16)), 0.0)
        for lvl in range(GLA_LEVELS):
            xl = ex[(2 + lvl) * c:(3 + lvl) * c]
            p = _dot_nt((q * xl).astype(BF16), (k * xl).astype(BF16))
            a = a + jnp.where(masks[lvl], p, 0.0)
        o = o + _dot(a.astype(BF16), v)

        ks = (k * ex[c:2 * c]).astype(BF16)
        decay = ex[c - 1:c]
        st_ref[...] = st * decay + _dot_tn(v, ks)

        rr = r_ref[rows, :].astype(F32)
        on = o * lax.rsqrt(jnp.mean(o * o, axis=-1, keepdims=True) + EPS) * nw
        o_ref[rows, :] = (on * _silu(rr)).astype(BF16)
        return carry

    lax.fori_loop(0, n_chunks, chunk, 0)


def _gla(u, small, w2, b2, nw, mat, batch, seq, tb):
    m = batch * seq
    nt = seq // tb
    row = lambda b, h, t: b * nt + t
    kern = functools.partial(_gla_kernel, n_chunks=tb // GLA_CHUNK)
    return pl.pallas_call(
        kern,
        grid=(batch, GLA_HEADS, nt),
        in_specs=[
            pl.BlockSpec((tb, GLA_DK), lambda b, h, t: (row(b, h, t), OFF_Q // GLA_DK + h)),
            pl.BlockSpec((tb, GLA_DK), lambda b, h, t: (row(b, h, t), OFF_K // GLA_DK + h)),
            pl.BlockSpec((tb, GLA_DV), lambda b, h, t: (row(b, h, t), OFF_V // GLA_DV + h)),
            pl.BlockSpec((tb, GLA_DV), lambda b, h, t: (row(b, h, t), OFF_R // GLA_DV + h)),
            pl.BlockSpec((tb, SMALL_COLS), lambda b, h, t: (row(b, h, t), 0)),
            pl.BlockSpec((GLA_RANK, GLA_DK), lambda b, h, t: (0, h)),
            pl.BlockSpec((1, GLA_DK), lambda b, h, t: (0, h)),
            pl.BlockSpec((1, GLA_DV), lambda b, h, t: (0, 0)),
            pl.BlockSpec((GLA_SECTIONS * GLA_CHUNK, GLA_CHUNK), lambda b, h, t: (0, 0)),
        ],
        out_specs=pl.BlockSpec((tb, GLA_DV), lambda b, h, t: (row(b, h, t), h)),
        out_shape=jax.ShapeDtypeStruct((m, GLA_VAL), BF16),
        scratch_shapes=[pltpu.VMEM((GLA_DV, GLA_DK), F32)],
        compiler_params=_params(("parallel", "parallel", "arbitrary")),
        name="gla_mixer",
    )(u, u, u, u, small, w2, b2, nw, mat)


def _ssd_constants():
    ex = np.zeros((SSM_GROUPS, SMALL_COLS, SSM_GW + SMALL_COLS), np.float32)
    for g in range(SSM_GROUPS):
        for j in range(SSM_HPG):
            lane = DT_LANE0 + g * SSM_HPG + j
            ex[g, lane, j * SSM_P:(j + 1) * SSM_P] = 1.0
            ex[g, lane, SSM_GW + j] = 1.0
    ex = np.concatenate([ex, ex], axis=1)
    tril = np.tril(np.ones((SSM_CHUNK, SSM_CHUNK), np.float32))
    return jnp.asarray(ex, BF16), jnp.asarray(tril, BF16)


def _ssd_kernel(z_ref, x_ref, bm_ref, cm_ref, s_ref,
                cwx_ref, cwb_ref, cwc_ref, cbx_ref, cbb_ref, cbc_ref,
                dtb_ref, aexp_ref, aloc_ref, dexp_ref, nw_ref, ex_ref, tril_ref,
                o_ref, xs_ref, bs_ref, cs_ref, st_ref, *, n_chunks, tb):
    ln = SSM_CHUNK
    tr = TAIL_ROWS

    @pl.when(pl.program_id(2) == 0)
    def _():
        st_ref[...] = jnp.zeros_like(st_ref)
        xs_ref[0:tr, :] = jnp.zeros((tr, SSM_GW), F32)
        bs_ref[0:tr, :] = jnp.zeros((tr, SSM_N), F32)
        cs_ref[0:tr, :] = jnp.zeros((tr, SSM_N), F32)

    xs_ref[tr:tr + tb, :] = x_ref[...].astype(F32)
    bs_ref[tr:tr + tb, :] = bm_ref[...].astype(F32)
    cs_ref[tr:tr + tb, :] = cm_ref[...].astype(F32)

    tril = tril_ref[...]
    exm = ex_ref[...]
    dtb = dtb_ref[...]
    a_exp = aexp_ref[...]
    a_loc = aloc_ref[...]
    d_exp = dexp_ref[...]
    nw = nw_ref[...]
    ii = lax.broadcasted_iota(jnp.int32, (ln, ln), 0)
    jj = lax.broadcasted_iota(jnp.int32, (ln, ln), 1)
    causal = jj <= ii

    def conv(src_ref, w_ref, b_ref, r0):
        acc = b_ref[...]
        for kk in range(SSM_CONV):
            off = tr + r0 - (SSM_CONV - 1) + kk
            acc = acc + src_ref[off:off + ln, :] * w_ref[kk:kk + 1, :]
        return _silu(acc)

    for ci in range(n_chunks):
        r0 = ci * ln
        xc = conv(xs_ref, cwx_ref, cbx_ref, r0)
        bc = conv(bs_ref, cwb_ref, cbb_ref, r0).astype(BF16)
        cc = conv(cs_ref, cwc_ref, cbc_ref, r0).astype(BF16)

        dt_small = _softplus(s_ref[r0:r0 + ln, :] + dtb)
        d_hi, d_lo = _split_hi_lo(dt_small)
        dtx = _dot(jnp.concatenate([d_hi, d_lo], axis=1), exm)
        dt_exp = dtx[:, :SSM_GW]
        dt_loc = dtx[:, SSM_GW:]

        ae_hi, ae_lo = _split_hi_lo(dt_exp * a_exp)
        c2 = _dot(tril, jnp.concatenate([ae_hi, ae_lo], axis=1))
        cum = c2[:, :SSM_GW] + c2[:, SSM_GW:]
        al_hi, al_lo = _split_hi_lo(dt_loc * a_loc)
        c3 = _dot(tril, jnp.concatenate([al_hi, al_lo], axis=1))
        cum_loc = c3[:, :SMALL_COLS] + c3[:, SMALL_COLS:]
        cum_t = cum_loc.T

        ecum = jnp.exp(cum)
        cum_last = cum[ln - 1:ln, :]
        xdt = xc * dt_exp
        st = st_ref[...]
        y = _dot(cc, st.astype(BF16)) * ecum
        cb = _dot_nt(cc, bc)
        yd = []
        for j in range(SSM_HPG):
            seg = cum_loc[:, j:j + 1] - cum_t[j:j + 1, :]
            lm = jnp.exp(jnp.where(causal, seg, -jnp.inf))
            w = (cb * lm).astype(BF16)
            yd.append(_dot(w, xdt[:, j * SSM_P:(j + 1) * SSM_P].astype(BF16)))
        y = y + jnp.concatenate(yd, axis=1)

        xd = (xdt * jnp.exp(cum_last - cum)).astype(BF16)
        st_ref[...] = st * ecum[ln - 1:ln, :] + _dot_tn(bc, xd)

        y = y + xc * d_exp
        y = y * _silu(z_ref[r0:r0 + ln, :].astype(F32))
        y = y * lax.rsqrt(jnp.mean(y * y, axis=-1, keepdims=True) + EPS) * nw
        o_ref[r0:r0 + ln, :] = y.astype(BF16)

    xs_ref[0:tr, :] = xs_ref[tb:tb + tr, :]
    bs_ref[0:tr, :] = bs_ref[tb:tb + tr, :]
    cs_ref[0:tr, :] = cs_ref[tb:tb + tr, :]


def _ssd(u, small, cw, cb, dtb_row, a_exp, a_loc, d_exp, nw, exm, tril, batch, seq, tb):
    m = batch * seq
    nt = seq // tb
    row = lambda b, g, t: b * nt + t
    gcol = lambda off, width: (lambda b, g, t: (row(b, g, t), off // width + g))
    xb = SSM_INNER // SSM_N
    kern = functools.partial(_ssd_kernel, n_chunks=tb // SSM_CHUNK, tb=tb)
    return pl.pallas_call(
        kern,
        grid=(batch, SSM_GROUPS, nt),
        in_specs=[
            pl.BlockSpec((tb, SSM_GW), gcol(OFF_Z, SSM_GW)),
            pl.BlockSpec((tb, SSM_GW), gcol(OFF_X, SSM_GW)),
            pl.BlockSpec((tb, SSM_N), gcol(OFF_B, SSM_N)),
            pl.BlockSpec((tb, SSM_N), gcol(OFF_C, SSM_N)),
            pl.BlockSpec((tb, SMALL_COLS), lambda b, g, t: (row(b, g, t), 0)),
            pl.BlockSpec((SSM_CONV, SSM_GW), lambda b, g, t: (0, g)),
            pl.BlockSpec((SSM_CONV, SSM_N), lambda b, g, t: (0, xb + g)),
            pl.BlockSpec((SSM_CONV, SSM_N), lambda b, g, t: (0, xb + SSM_GROUPS + g)),
            pl.BlockSpec((1, SSM_GW), lambda b, g, t: (0, g)),
            pl.BlockSpec((1, SSM_N), lambda b, g, t: (0, xb + g)),
            pl.BlockSpec((1, SSM_N), lambda b, g, t: (0, xb + SSM_GROUPS + g)),
            pl.BlockSpec((1, SMALL_COLS), lambda b, g, t: (0, 0)),
            pl.BlockSpec((1, SSM_GW), lambda b, g, t: (0, g)),
            pl.BlockSpec((None, 1, SMALL_COLS), lambda b, g, t: (g, 0, 0)),
            pl.BlockSpec((1, SSM_GW), lambda b, g, t: (0, g)),
            pl.BlockSpec((1, SSM_GW), lambda b, g, t: (0, g)),
            pl.BlockSpec((None, 2 * SMALL_COLS, SSM_GW + SMALL_COLS), lambda b, g, t: (g, 0, 0)),
            pl.BlockSpec((SSM_CHUNK, SSM_CHUNK), lambda b, g, t: (0, 0)),
        ],
        out_specs=pl.BlockSpec((tb, SSM_GW), lambda b, g, t: (row(b, g, t), g)),
        out_shape=jax.ShapeDtypeStruct((m, SSM_INNER), BF16),
        scratch_shapes=[pltpu.VMEM((tb + TAIL_ROWS, SSM_GW), F32),
                        pltpu.VMEM((tb + TAIL_ROWS, SSM_N), F32),
                        pltpu.VMEM((tb + TAIL_ROWS, SSM_N), F32),
                        pltpu.VMEM((SSM_N, SSM_GW), F32)],
        compiler_params=_params(("parallel", "parallel", "arbitrary")),
        name="ssd_mixer",
    )(u, u, u, u, small, cw, cw, cw, cb, cb, cb, dtb_row, a_exp, a_loc, d_exp, nw, exm, tril)


def _merge_kernel(oa_ref, ob_ref, ga_ref, gb_ref, x_ref, wa_ref, wb_ref, wo_ref, nw_ref,
                  xo_ref, h_ref):
    ya = _dot(oa_ref[...], wa_ref[...])
    yb = _dot(ob_ref[...], wb_ref[...])
    mix = _sigmoid(ga_ref[...].astype(F32)) * ya + _sigmoid(gb_ref[...].astype(F32)) * yb
    xn = x_ref[...] + _dot(mix.astype(BF16), wo_ref[...])
    xo_ref[...] = xn
    h_ref[...] = _rms_rows(xn, nw_ref[...]).astype(BF16)


def _merge(oa, ob, u, x, wa, wb, wo, nw, tm):
    m = x.shape[0]
    full = lambda shape: pl.BlockSpec(shape, lambda i: (0, 0))
    return pl.pallas_call(
        _merge_kernel,
        grid=(m // tm,),
        in_specs=[pl.BlockSpec((tm, GLA_VAL), lambda i: (i, 0)),
                  pl.BlockSpec((tm, SSM_INNER), lambda i: (i, 0)),
                  pl.BlockSpec((tm, D_MODEL), lambda i: (i, OFF_GA // D_MODEL)),
                  pl.BlockSpec((tm, D_MODEL), lambda i: (i, OFF_GB // D_MODEL)),
                  pl.BlockSpec((tm, D_MODEL), lambda i: (i, 0)),
                  full((GLA_VAL, D_MODEL)), full((SSM_INNER, D_MODEL)),
                  full((D_MODEL, D_MODEL)), full((1, D_MODEL))],
        out_specs=[pl.BlockSpec((tm, D_MODEL), lambda i: (i, 0)),
                   pl.BlockSpec((tm, D_MODEL), lambda i: (i, 0))],
        out_shape=[jax.ShapeDtypeStruct((m, D_MODEL), F32),
                   jax.ShapeDtypeStruct((m, D_MODEL), BF16)],
        compiler_params=_params(("parallel",)),
        name="merge_out",
    )(oa, ob, u, u, x, wa, wb, wo, nw)


def _ffn_kernel(h_ref, x_ref, wg_ref, wu_ref, wo_ref, nw_ref, o_ref, *, last):
    h = h_ref[...]
    gate = _dot(h, wg_ref[...])
    up = _dot(h, wu_ref[...])
    act = (_silu(gate) * up).astype(BF16)
    xn = x_ref[...] + _dot(act, wo_ref[...])
    if last:
        o_ref[...] = _rms_rows(xn, nw_ref[...])
    else:
        o_ref[0][...] = xn
        o_ref[1][...] = _rms_rows(xn, nw_ref[...]).astype(BF16)


def _ffn_kernel_mid(h_ref, x_ref, wg_ref, wu_ref, wo_ref, nw_ref, xo_ref, ho_ref):
    _ffn_kernel(h_ref, x_ref, wg_ref, wu_ref, wo_ref, nw_ref, (xo_ref, ho_ref), last=False)


def _ffn_kernel_last(h_ref, x_ref, wg_ref, wu_ref, wo_ref, nw_ref, o_ref):
    _ffn_kernel(h_ref, x_ref, wg_ref, wu_ref, wo_ref, nw_ref, o_ref, last=True)


def _ffn(h, x, wg, wu, wo, nw, tm, last):
    m = x.shape[0]
    full = lambda shape: pl.BlockSpec(shape, lambda i: (0, 0))
    rows = pl.BlockSpec((tm, D_MODEL), lambda i: (i, 0))
    if last:
        kern, out_specs = _ffn_kernel_last, rows
        out_shape = jax.ShapeDtypeStruct((m, D_MODEL), F32)
    else:
        kern, out_specs = _ffn_kernel_mid, [rows, rows]
        out_shape = [jax.ShapeDtypeStruct((m, D_MODEL), F32),
                     jax.ShapeDtypeStruct((m, D_MODEL), BF16)]
    return pl.pallas_call(
        kern,
        grid=(m // tm,),
        in_specs=[rows, rows, full((D_MODEL, FFN_HIDDEN)), full((D_MODEL, FFN_HIDDEN)),
                  full((FFN_HIDDEN, D_MODEL)), full((1, D_MODEL))],
        out_specs=out_specs,
        out_shape=out_shape,
        compiler_params=_params(("parallel",)),
        name="ffn_last" if last else "ffn",
    )(h, x, wg, wu, wo, nw)


def _tiles(batch, seq):
    m = batch * seq
    pick = lambda n, cands: next(c for c in cands if n % c == 0)
    return dict(
        norm=pick(m, (1024, 512, 256, 128)),
        inproj_m=pick(m, (1024, 512, 256, 128)),
        inproj_n=1024,
        gla=pick(seq, (1024, 512, 256, 128, 64)),
        ssd=pick(seq, (512, 256, 128)),
        merge=pick(m, (256, 128)),
        ffn=pick(m, (256, 128)),
    )


def kernel(x, norm1_w, w_in, gla_gate_w2, gla_gate_b, gla_norm_w, ssm_conv_w, ssm_conv_b,
           ssm_dt_bias, ssm_A_log, ssm_D, ssm_norm_w, w_branch_a, w_branch_b, w_mix_out,
           norm2_w, w_ffn_in, w_ffn_out, final_norm_w):
    batch, seq, _ = x.shape
    depth = w_in.shape[0]
    m = batch * seq
    tl = _tiles(batch, seq)
    gla_mat = _gla_constants()
    ssd_ex, ssd_tril = _ssd_constants()

    offs = np.cumsum((0,) + IN_SPLITS)
    o_glr, o_z, o_dt, o_ga = offs[4], offs[5], offs[7], offs[8]

    xf = x.reshape(m, D_MODEL)
    h = _norm(xf, norm1_w[0].reshape(1, D_MODEL), tl["norm"])
    for l in range(depth):
        w = w_in[l]
        w_main = jnp.concatenate(
            [w[:, :o_glr], w[:, o_z:o_dt], w[:, o_ga:]], axis=1).astype(BF16)
        w_small = jnp.concatenate(
            [w[:, o_glr:o_z], w[:, o_dt:o_ga],
             jnp.zeros((D_MODEL, SMALL_COLS - GLA_RANK - SSM_HEADS), F32)], axis=1).astype(BF16)
        u, small = _inproj(h, w_main, w_small, tl["inproj_m"], tl["inproj_n"])

        oa = _gla(u, small, gla_gate_w2[l], gla_gate_b[l].reshape(1, GLA_KEY),
                  gla_norm_w[l].reshape(1, GLA_DV), gla_mat, batch, seq, tl["gla"])

        a_neg = -jnp.exp(ssm_A_log[l])
        pad_l = jnp.zeros((DT_LANE0,), F32)
        pad_r = jnp.zeros((SMALL_COLS - DT_LANE0 - SSM_HEADS,), F32)
        dtb_row = jnp.concatenate([pad_l, ssm_dt_bias[l], pad_r]).reshape(1, SMALL_COLS)
        a_exp = jnp.repeat(a_neg, SSM_P).reshape(1, SSM_INNER)
        a_loc = jnp.pad(a_neg.reshape(SSM_GROUPS, 1, SSM_HPG),
                        ((0, 0), (0, 0), (0, SMALL_COLS - SSM_HPG)))
        d_exp = jnp.repeat(ssm_D[l], SSM_P).reshape(1, SSM_INNER)
        ob = _ssd(u, small, ssm_conv_w[l], ssm_conv_b[l].reshape(1, SSM_XBC), dtb_row,
                  a_exp, a_loc, d_exp, ssm_norm_w[l].reshape(1, SSM_INNER),
                  ssd_ex, ssd_tril, batch, seq, tl["ssd"])

        xf, h2 = _merge(oa, ob, u, xf, w_branch_a[l].astype(BF16), w_branch_b[l].astype(BF16),
                        w_mix_out[l].astype(BF16), norm2_w[l].reshape(1, D_MODEL), tl["merge"])

        wf = w_ffn_in[l]
        wg, wu = wf[:, :FFN_HIDDEN].astype(BF16), wf[:, FFN_HIDDEN:].astype(BF16)
        wo = w_ffn_out[l].astype(BF16)
        if l + 1 < depth:
            xf, h = _ffn(h2, xf, wg, wu, wo, norm1_w[l + 1].reshape(1, D_MODEL), tl["ffn"], False)
        else:
            xf = _ffn(h2, xf, wg, wu, wo, final_norm_w.reshape(1, D_MODEL), tl["ffn"], True)
    return xf.reshape(batch, seq, D_MODEL)
```

```python
import functools

import numpy as np
import jax
import jax.numpy as jnp
from jax import lax
from jax.experimental import pallas as pl
from jax.experimental.pallas import tpu as pltpu

F32 = jnp.float32
BF16 = jnp.bfloat16

D_MODEL = 1024
EPS = 1e-6
GLA_HEADS = 4
GLA_DK = 128
GLA_DV = 256
GLA_KEY = GLA_HEADS * GLA_DK
GLA_VAL = GLA_HEADS * GLA_DV
GLA_RANK = 16
GLA_TAU = 16.0
GLA_CHUNK = 64
GLA_LEVELS = 6
GLA_MXU_LEVELS = 2
SSM_INNER = 2048
SSM_P = 64
SSM_HEADS = 32
SSM_GROUPS = 4
SSM_N = 128
SSM_CONV = 4
SSM_CHUNK = 128
SSM_HPG = SSM_HEADS // SSM_GROUPS
SSM_GW = SSM_HPG * SSM_P
SSM_CW = SSM_GW + 2 * SSM_N
SSM_XBC = SSM_INNER + 2 * SSM_GROUPS * SSM_N
FFN_HIDDEN = 2816
IN_SPLITS = (GLA_KEY, GLA_KEY, GLA_VAL, GLA_VAL, GLA_RANK, SSM_INNER, SSM_XBC,
             SSM_HEADS, D_MODEL, D_MODEL)

OFF_Q = 0
OFF_K = OFF_Q + GLA_KEY
OFF_V = OFF_K + GLA_KEY
OFF_R = OFF_V + GLA_VAL
OFF_Z = OFF_R + GLA_VAL
OFF_X = OFF_Z + SSM_INNER
OFF_B = OFF_X + SSM_INNER
OFF_C = OFF_B + SSM_GROUPS * SSM_N
OFF_GA = OFF_C + SSM_GROUPS * SSM_N
OFF_GB = OFF_GA + D_MODEL
U_COLS = OFF_GB + D_MODEL
SMALL_COLS = 128
DT_LANE0 = GLA_RANK
HIST_ROWS = 16

VMEM_LIMIT = 48 * 1024 * 1024


def _sigmoid(x):
    return 0.5 * jnp.tanh(0.5 * x) + 0.5


def _silu(x):
    hx = 0.5 * x
    return hx * jnp.tanh(hx) + hx


def _softplus(x):
    return jnp.maximum(x, 0.0) + jnp.log1p(jnp.exp(-jnp.abs(x)))


def _log_sigmoid(x):
    return jnp.minimum(x, 0.0) - jnp.log1p(jnp.exp(-jnp.abs(x)))


def _split_hi_lo(x):
    hi = x.astype(BF16)
    lo = (x - hi.astype(F32)).astype(BF16)
    return hi, lo


def _hi_lo(x):
    hi, lo = _split_hi_lo(x)
    return jnp.concatenate([hi, lo], axis=1)


def _dot(a, b):
    return jnp.dot(a, b, preferred_element_type=F32)


def _dot_nt(a, b):
    return lax.dot_general(a, b, (((1,), (1,)), ((), ())), preferred_element_type=F32)


def _dot_tn(a, b):
    return lax.dot_general(a, b, (((0,), (0,)), ((), ())), preferred_element_type=F32)


def _params(semantics):
    return pltpu.CompilerParams(dimension_semantics=semantics, vmem_limit_bytes=VMEM_LIMIT)


def _rms_rows(x, w):
    return x * lax.rsqrt(jnp.mean(x * x, axis=-1, keepdims=True) + EPS) * w


def _norm_kernel(x_ref, w_ref, h_ref):
    h_ref[...] = _rms_rows(x_ref[...], w_ref[...]).astype(BF16)


def _norm(x, w, tm):
    m = x.shape[0]
    return pl.pallas_call(
        _norm_kernel,
        grid=(m // tm,),
        in_specs=[pl.BlockSpec((tm, D_MODEL), lambda i: (i, 0)),
                  pl.BlockSpec((1, D_MODEL), lambda i: (0, 0))],
        out_specs=pl.BlockSpec((tm, D_MODEL), lambda i: (i, 0)),
        out_shape=jax.ShapeDtypeStruct((m, D_MODEL), BF16),
        compiler_params=_params(("parallel",)),
        name="rms_norm",
    )(x, w)


def _inproj_kernel(h_ref, w_ref, ws_ref, u_ref, s_ref):
    h = h_ref[...]
    u_ref[...] = _dot(h, w_ref[...]).astype(BF16)

    @pl.when(pl.program_id(1) == 0)
    def _():
        s_ref[...] = _dot(h, ws_ref[...])


def _inproj(h, w_main, w_small, tm, tn):
    m = h.shape[0]
    return pl.pallas_call(
        _inproj_kernel,
        grid=(m // tm, U_COLS // tn),
        in_specs=[pl.BlockSpec((tm, D_MODEL), lambda i, j: (i, 0)),
                  pl.BlockSpec((D_MODEL, tn), lambda i, j: (0, j)),
                  pl.BlockSpec((D_MODEL, SMALL_COLS), lambda i, j: (0, 0))],
        out_specs=[pl.BlockSpec((tm, tn), lambda i, j: (i, j)),
                   pl.BlockSpec((tm, SMALL_COLS), lambda i, j: (i, 0))],
        out_shape=[jax.ShapeDtypeStruct((m, U_COLS), BF16),
                   jax.ShapeDtypeStruct((m, SMALL_COLS), F32)],
        compiler_params=_params(("parallel", "arbitrary")),
        name="in_proj",
    )(h, w_main, w_small)


def _gla_constants():
    c = GLA_CHUNK
    mat = np.zeros((1 + GLA_MXU_LEVELS, c, c), np.float32)
    t = np.arange(c)[:, None]
    tau = np.arange(c)[None, :]
    mat[0] = tau <= t
    for lvl in range(GLA_MXU_LEVELS):
        s = 1 << lvl
        mid = (t // (2 * s)) * (2 * s) + s
        upper = t >= mid
        mat[1 + lvl] = np.where(upper, (tau >= mid) & (tau <= t), (tau > t) & (tau <= mid - 1))
    return jnp.asarray(mat.reshape((1 + GLA_MXU_LEVELS) * c, c), BF16)


def _gla_level_mask(lvl):
    c = GLA_CHUNK
    s = 1 << lvl
    i = lax.broadcasted_iota(jnp.int32, (c, c), 0)
    j = lax.broadcasted_iota(jnp.int32, (c, c), 1)
    same = (i // (2 * s)) == (j // (2 * s))
    return same & ((i % (2 * s)) >= s) & ((j % (2 * s)) < s)


def _gla_kernel(q_ref, k_ref, v_ref, r_ref, s_ref, w2_ref, b2_ref, nw_ref, mat_ref,
                o_ref, st_ref, *, n_chunks):
    c = GLA_CHUNK

    @pl.when(pl.program_id(1) == 0)
    def _():
        st_ref[...] = jnp.zeros_like(st_ref)

    w2 = w2_ref[...]
    b2 = b2_ref[...]
    nw = nw_ref[...]
    mat = mat_ref[...]
    ii = lax.broadcasted_iota(jnp.int32, (c, c), 0)
    jj = lax.broadcasted_iota(jnp.int32, (c, c), 1)
    masks = [_gla_level_mask(lvl) for lvl in range(GLA_LEVELS)]
    eye = ii == jj
    heads = range(GLA_HEADS)
    kcs = [slice(h * GLA_DK, (h + 1) * GLA_DK) for h in heads]
    vcs = [slice(h * GLA_DV, (h + 1) * GLA_DV) for h in heads]

    def ref_rows(b, s):
        blk = b.reshape(c // (2 * s), 2 * s, GLA_DK)
        return jnp.broadcast_to(blk[:, s - 1:s, :], blk.shape).reshape(c, GLA_DK)

    def chunk(ci, carry):
        r0 = pl.multiple_of(ci * c, c)
        rows = pl.ds(r0, c)
        l_hi, l_lo = _split_hi_lo(s_ref[rows, :])
        logit = _dot(jnp.concatenate([l_hi, l_lo, l_hi], axis=1), w2) + b2
        g_all = _log_sigmoid(logit) * (1.0 / GLA_TAU)

        x_inter, x_state, x_lvl, d_col = [], [], [], []
        for h in heads:
            e2 = _dot(mat, _hi_lo(g_all[:, kcs[h]]))
            e = e2[:, :GLA_DK] + e2[:, GLA_DK:]
            b = e[0:c]
            x_inter.append(jnp.exp(b))
            x_state.append(jnp.exp(b[c - 1:c] - b))
            lv = [jnp.exp(e[(1 + l) * c:(2 + l) * c]) for l in range(GLA_MXU_LEVELS)]
            lv += [jnp.exp(-jnp.abs(b - ref_rows(b, 1 << l)))
                   for l in range(GLA_MXU_LEVELS, GLA_LEVELS)]
            x_lvl.append(lv)
            d_col.append(jnp.exp(b.T[:, c - 1:c]))

        amat = []
        for h in heads:
            q = q_ref[rows, kcs[h]].astype(F32) * (GLA_DK ** -0.5)
            k = k_ref[rows, kcs[h]].astype(F32)
            a = jnp.where(eye, _dot_nt(q.astype(BF16), k.astype(BF16)), 0.0)
            for lvl in range(GLA_LEVELS):
                xl = x_lvl[h][lvl]
                p = _dot_nt((q * xl).astype(BF16), (k * xl).astype(BF16))
                a = a + jnp.where(masks[lvl], p, 0.0)
            amat.append(a.astype(BF16))

        outs = []
        for h in heads:
            q = q_ref[rows, kcs[h]].astype(F32) * (GLA_DK ** -0.5)
            k = k_ref[rows, kcs[h]].astype(F32)
            v = v_ref[rows, vcs[h]]
            st = st_ref[h]
            outs.append(_dot((q * x_inter[h]).astype(BF16), st.astype(BF16)) + _dot(amat[h], v))
            ks = (k * x_state[h]).astype(BF16)
            st_ref[h] = st * d_col[h] + _dot_tn(ks, v)

        for h in heads:
            o = outs[h]
            rr = r_ref[rows, vcs[h]].astype(F32)
            on = o * lax.rsqrt(jnp.mean(o * o, axis=-1, keepdims=True) + EPS) * nw
            o_ref[rows, vcs[h]] = (on * _silu(rr)).astype(BF16)
        return carry

    lax.fori_loop(0, n_chunks, chunk, 0)


def _gla(u, small, w2, b2, nw, mat, batch, seq, tb):
    m = batch * seq
    nt = seq // tb
    row = lambda b, t: b * nt + t
    kern = functools.partial(_gla_kernel, n_chunks=tb // GLA_CHUNK)
    return pl.pallas_call(
        kern,
        grid=(batch, nt),
        in_specs=[
            pl.BlockSpec((tb, GLA_KEY), lambda b, t: (row(b, t), OFF_Q // GLA_KEY)),
            pl.BlockSpec((tb, GLA_KEY), lambda b, t: (row(b, t), OFF_K // GLA_KEY)),
            pl.BlockSpec((tb, GLA_VAL), lambda b, t: (row(b, t), OFF_V // GLA_VAL)),
            pl.BlockSpec((tb, GLA_VAL), lambda b, t: (row(b, t), OFF_R // GLA_VAL)),
            pl.BlockSpec((tb, SMALL_COLS), lambda b, t: (row(b, t), 0)),
            pl.BlockSpec((3 * SMALL_COLS, GLA_KEY), lambda b, t: (0, 0)),
            pl.BlockSpec((1, GLA_KEY), lambda b, t: (0, 0)),
            pl.BlockSpec((1, GLA_DV), lambda b, t: (0, 0)),
            pl.BlockSpec(((1 + GLA_MXU_LEVELS) * GLA_CHUNK, GLA_CHUNK), lambda b, t: (0, 0)),
        ],
        out_specs=pl.BlockSpec((tb, GLA_VAL), lambda b, t: (row(b, t), 0)),
        out_shape=jax.ShapeDtypeStruct((m, GLA_VAL), BF16),
        scratch_shapes=[pltpu.VMEM((GLA_HEADS, GLA_DK, GLA_DV), F32)],
        compiler_params=_params(("parallel", "arbitrary")),
        name="gla_mixer",
    )(u, u, u, u, small, w2, b2, nw, mat)


def _ssd_constants():
    ex = np.zeros((SSM_GROUPS, SMALL_COLS, SSM_GW), np.float32)
    for g in range(SSM_GROUPS):
        for j in range(SSM_HPG):
            ex[g, DT_LANE0 + g * SSM_HPG + j, j * SSM_P:(j + 1) * SSM_P] = 1.0
    ln = SSM_CHUNK
    tril = np.tril(np.ones((ln, ln), np.float32))
    tri2 = np.concatenate([tril, 1.0 - tril], axis=0)
    shift = np.zeros((SSM_CONV, ln, ln + HIST_ROWS), np.float32)
    for k in range(SSM_CONV):
        shift[k, np.arange(ln), HIST_ROWS + np.arange(ln) - (SSM_CONV - 1) + k] = 1.0
    return (jnp.asarray(ex, BF16), jnp.asarray(tri2, BF16),
            jnp.asarray(shift.reshape(SSM_CONV * ln, ln + HIST_ROWS), BF16))


def _ssd_kernel(z_ref, x_ref, bm_ref, cm_ref, xh_ref, bh_ref, ch_ref, s_ref, cw_ref, cb_ref,
                dtb_ref, arow_ref, dexp_ref, nw_ref, ex_ref, tri2_ref, shift_ref,
                o_ref, st_ref, *, n_chunks, tb):
    ln = SSM_CHUNK
    hr = HIST_ROWS
    gw = SSM_GW
    first = pl.program_id(2) == 0

    @pl.when(first)
    def _():
        st_ref[...] = jnp.zeros_like(st_ref)

    def raw_rows(lo, hi):
        return jnp.concatenate([x_ref[lo:hi, :], bm_ref[lo:hi, :], cm_ref[lo:hi, :]], axis=1)

    hist = jnp.concatenate([xh_ref[...], bh_ref[...], ch_ref[...]], axis=1)
    hist = jnp.where(first, jnp.zeros_like(hist), hist)

    tri2 = tri2_ref[...]
    shift = shift_ref[...]
    exm = ex_ref[...]
    cw = cw_ref[...]
    cbias = cb_ref[...]
    dtb = dtb_ref[...]
    a_row = arow_ref[...]
    d_exp = dexp_ref[...]
    nw = nw_ref[...]
    ii = lax.broadcasted_iota(jnp.int32, (ln, ln), 0)
    jj = lax.broadcasted_iota(jnp.int32, (ln, ln), 1)
    causal = jj <= ii
    low_half = lax.broadcasted_iota(jnp.int32, (ln, 2 * SSM_P), 1) < SSM_P
    loc_shift = SMALL_COLS - DT_LANE0 - pl.program_id(1) * SSM_HPG

    def prep(ci):
        r0 = ci * ln
        if ci == 0:
            ext = jnp.concatenate([hist, raw_rows(0, ln)], axis=0)
        else:
            ext = raw_rows(r0 - hr, r0 + ln)
        taps = _dot(shift, ext)
        acc = cbias
        for k in range(SSM_CONV):
            acc = acc + taps[k * ln:(k + 1) * ln] * cw[k:k + 1, :]
        xbc = _silu(acc)
        xc = xbc[:, :gw]
        bc = xbc[:, gw:gw + SSM_N].astype(BF16)
        cc = xbc[:, gw + SSM_N:].astype(BF16)

        dt_small = _softplus(s_ref[r0:r0 + ln, :] + dtb)
        cs = _dot(tri2, _hi_lo(dt_small * a_row))
        cum_small = cs[:ln, :SMALL_COLS] + cs[:ln, SMALL_COLS:]
        rcum_small = cs[ln:, :SMALL_COLS] + cs[ln:, SMALL_COLS:]
        small3 = jnp.concatenate([dt_small.astype(BF16), jnp.exp(cum_small).astype(BF16),
                                  jnp.exp(rcum_small).astype(BF16)], axis=0)
        wide3 = _dot(small3, exm)
        cum_loc = pltpu.roll(cum_small, loc_shift, axis=1)
        cum_t = cum_loc.T
        return xc, bc, cc, wide3[:ln], wide3[ln:2 * ln], wide3[2 * ln:], cum_loc, cum_t

    def finish(ci, front):
        r0 = ci * ln
        xc, bc, cc, dt_exp, ecum, edst, cum_loc, cum_t = front
        xdt = xc * dt_exp
        xdt_b = xdt.astype(BF16)
        st = st_ref[...]
        y_off = _dot(cc, st.astype(BF16)) * ecum
        cbm = jnp.where(causal, _dot_nt(cc, bc), 0.0)
        yd = []
        for p in range(SSM_HPG // 2):
            ws, xm = [], []
            xp = xdt_b[:, 2 * p * SSM_P:(2 * p + 2) * SSM_P]
            for j, keep in ((2 * p, low_half), (2 * p + 1, ~low_half)):
                seg = cum_loc[:, j:j + 1] - cum_t[j:j + 1, :]
                ws.append((cbm * jnp.exp(jnp.minimum(seg, 0.0))).astype(BF16))
                xm.append(jnp.where(keep, xp, jnp.zeros_like(xp)))
            yd.append(_dot(jnp.concatenate(ws, axis=1), jnp.concatenate(xm, axis=0)))
        y = y_off + jnp.concatenate(yd, axis=1)

        xd = (xdt * edst).astype(BF16)
        st_ref[...] = st * ecum[ln - 1:ln, :] + _dot_tn(bc, xd)

        y = y + xc * d_exp
        y = y * _silu(z_ref[r0:r0 + ln, :].astype(F32))
        y = y * lax.rsqrt(jnp.mean(y * y, axis=-1, keepdims=True) + EPS) * nw
        o_ref[r0:r0 + ln, :] = y.astype(BF16)

    front = prep(0)
    for ci in range(n_chunks):
        nxt = prep(ci + 1) if ci + 1 < n_chunks else None
        finish(ci, front)
        front = nxt


def _ssd(u, small, cw, cb, dtb_row, a_row, d_exp, nw, exm, tri2, shift, batch, seq, tb):
    m = batch * seq
    nt = seq // tb
    row = lambda b, g, t: b * nt + t
    gcol = lambda off, width: (lambda b, g, t: (row(b, g, t), off // width + g))
    hist_row = lambda b, g, t: jnp.maximum(row(b, g, t) * (tb // HIST_ROWS) - 1, 0)
    hcol = lambda off, width: (lambda b, g, t: (hist_row(b, g, t), off // width + g))
    const2 = lambda shape: pl.BlockSpec(shape, lambda b, g, t: (0, 0))
    per_group = lambda r, c: pl.BlockSpec((None, r, c), lambda b, g, t: (g, 0, 0))
    kern = functools.partial(_ssd_kernel, n_chunks=tb // SSM_CHUNK, tb=tb)
    return pl.pallas_call(
        kern,
        grid=(batch, SSM_GROUPS, nt),
        in_specs=[
            pl.BlockSpec((tb, SSM_GW), gcol(OFF_Z, SSM_GW)),
            pl.BlockSpec((tb, SSM_GW), gcol(OFF_X, SSM_GW)),
            pl.BlockSpec((tb, SSM_N), gcol(OFF_B, SSM_N)),
            pl.BlockSpec((tb, SSM_N), gcol(OFF_C, SSM_N)),
            pl.BlockSpec((HIST_ROWS, SSM_GW), hcol(OFF_X, SSM_GW)),
            pl.BlockSpec((HIST_ROWS, SSM_N), hcol(OFF_B, SSM_N)),
            pl.BlockSpec((HIST_ROWS, SSM_N), hcol(OFF_C, SSM_N)),
            pl.BlockSpec((tb, SMALL_COLS), lambda b, g, t: (row(b, g, t), 0)),
            per_group(SSM_CONV, SSM_CW),
            per_group(1, SSM_CW),
            const2((1, SMALL_COLS)),
            const2((1, SMALL_COLS)),
            pl.BlockSpec((1, SSM_GW), lambda b, g, t: (0, g)),
            pl.BlockSpec((1, SSM_GW), lambda b, g, t: (0, g)),
            per_group(SMALL_COLS, SSM_GW),
            const2((2 * SSM_CHUNK, SSM_CHUNK)),
            const2((SSM_CONV * SSM_CHUNK, SSM_CHUNK + HIST_ROWS)),
        ],
        out_specs=pl.BlockSpec((tb, SSM_GW), lambda b, g, t: (row(b, g, t), g)),
        out_shape=jax.ShapeDtypeStruct((m, SSM_INNER), BF16),
        scratch_shapes=[pltpu.VMEM((SSM_N, SSM_GW), F32)],
        compiler_params=_params(("parallel", "parallel", "arbitrary")),
        name="ssd_mixer",
    )(u, u, u, u, u, u, u, small, cw, cb, dtb_row, a_row, d_exp, nw, exm, tri2, shift)


def _merge_kernel(oa_ref, ob_ref, ga_ref, gb_ref, x_ref, wa_ref, wb_ref, wo_ref, nw_ref,
                  xo_ref, h_ref):
    ya = _dot(oa_ref[...], wa_ref[...])
    yb = _dot(ob_ref[...], wb_ref[...])
    mix = _sigmoid(ga_ref[...].astype(F32)) * ya + _sigmoid(gb_ref[...].astype(F32)) * yb
    xn = x_ref[...] + _dot(mix.astype(BF16), wo_ref[...])
    xo_ref[...] = xn
    h_ref[...] = _rms_rows(xn, nw_ref[...]).astype(BF16)


def _merge(oa, ob, u, x, wa, wb, wo, nw, tm):
    m = x.shape[0]
    full = lambda shape: pl.BlockSpec(shape, lambda i: (0, 0))
    return pl.pallas_call(
        _merge_kernel,
        grid=(m // tm,),
        in_specs=[pl.BlockSpec((tm, GLA_VAL), lambda i: (i, 0)),
                  pl.BlockSpec((tm, SSM_INNER), lambda i: (i, 0)),
                  pl.BlockSpec((tm, D_MODEL), lambda i: (i, OFF_GA // D_MODEL)),
                  pl.BlockSpec((tm, D_MODEL), lambda i: (i, OFF_GB // D_MODEL)),
                  pl.BlockSpec((tm, D_MODEL), lambda i: (i, 0)),
                  full((GLA_VAL, D_MODEL)), full((SSM_INNER, D_MODEL)),
                  full((D_MODEL, D_MODEL)), full((1, D_MODEL))],
        out_specs=[pl.BlockSpec((tm, D_MODEL), lambda i: (i, 0)),
                   pl.BlockSpec((tm, D_MODEL), lambda i: (i, 0))],
        out_shape=[jax.ShapeDtypeStruct((m, D_MODEL), F32),
                   jax.ShapeDtypeStruct((m, D_MODEL), BF16)],
        compiler_params=_params(("parallel",)),
        name="merge_out",
    )(oa, ob, u, u, x, wa, wb, wo, nw)


def _ffn_kernel(h_ref, x_ref, wg_ref, wu_ref, wo_ref, nw_ref, o_ref, *, last):
    h = h_ref[...]
    gate = _dot(h, wg_ref[...])
    up = _dot(h, wu_ref[...])
    act = (_silu(gate) * up).astype(BF16)
    xn = x_ref[...] + _dot(act, wo_ref[...])
    if last:
        o_ref[...] = _rms_rows(xn, nw_ref[...])
    else:
        o_ref[0][...] = xn
        o_ref[1][...] = _rms_rows(xn, nw_ref[...]).astype(BF16)


def _ffn_kernel_mid(h_ref, x_ref, wg_ref, wu_ref, wo_ref, nw_ref, xo_ref, ho_ref):
    _ffn_kernel(h_ref, x_ref, wg_ref, wu_ref, wo_ref, nw_ref, (xo_ref, ho_ref), last=False)


def _ffn_kernel_last(h_ref, x_ref, wg_ref, wu_ref, wo_ref, nw_ref, o_ref):
    _ffn_kernel(h_ref, x_ref, wg_ref, wu_ref, wo_ref, nw_ref, o_ref, last=True)


def _ffn(h, x, wg, wu, wo, nw, tm, last):
    m = x.shape[0]
    full = lambda shape: pl.BlockSpec(shape, lambda i: (0, 0))
    rows = pl.BlockSpec((tm, D_MODEL), lambda i: (i, 0))
    if last:
        kern, out_specs = _ffn_kernel_last, rows
        out_shape = jax.ShapeDtypeStruct((m, D_MODEL), F32)
    else:
        kern, out_specs = _ffn_kernel_mid, [rows, rows]
        out_shape = [jax.ShapeDtypeStruct((m, D_MODEL), F32),
                     jax.ShapeDtypeStruct((m, D_MODEL), BF16)]
    return pl.pallas_call(
        kern,
        grid=(m // tm,),
        in_specs=[rows, rows, full((D_MODEL, FFN_HIDDEN)), full((D_MODEL, FFN_HIDDEN)),
                  full((FFN_HIDDEN, D_MODEL)), full((1, D_MODEL))],
        out_specs=out_specs,
        out_shape=out_shape,
        compiler_params=_params(("parallel",)),
        name="ffn_last" if last else "ffn",
    )(h, x, wg, wu, wo, nw)


def _tiles(batch, seq):
    m = batch * seq
    pick = lambda n, cands: next(c for c in cands if n % c == 0)
    return dict(
        norm=pick(m, (1024, 512, 256, 128)),
        inproj_m=pick(m, (1024, 512, 256, 128)),
        inproj_n=1024,
        gla=pick(seq, (1024, 512, 256, 128, 64)),
        ssd=pick(seq, (512, 256, 128)),
        merge=pick(m, (256, 128)),
        ffn=pick(m, (256, 128)),
    )


def _ssd_group_cols(a):
    parts = []
    for g in range(SSM_GROUPS):
        b0 = SSM_INNER + g * SSM_N
        c0 = SSM_INNER + SSM_GROUPS * SSM_N + g * SSM_N
        parts.append(jnp.concatenate(
            [a[:, g * SSM_GW:(g + 1) * SSM_GW], a[:, b0:b0 + SSM_N], a[:, c0:c0 + SSM_N]], axis=1))
    return jnp.stack(parts)


def kernel(x, norm1_w, w_in, gla_gate_w2, gla_gate_b, gla_norm_w, ssm_conv_w, ssm_conv_b,
           ssm_dt_bias, ssm_A_log, ssm_D, ssm_norm_w, w_branch_a, w_branch_b, w_mix_out,
           norm2_w, w_ffn_in, w_ffn_out, final_norm_w):
    batch, seq, _ = x.shape
    depth = w_in.shape[0]
    m = batch * seq
    tl = _tiles(batch, seq)
    gla_mat = _gla_constants()
    ssd_ex, ssd_tri2, ssd_shift = _ssd_constants()

    offs = np.cumsum((0,) + IN_SPLITS)
    o_glr, o_z, o_dt, o_ga = offs[4], offs[5], offs[7], offs[8]

    def small_row(v):
        return jnp.pad(v, (DT_LANE0, SMALL_COLS - DT_LANE0 - SSM_HEADS)).reshape(1, SMALL_COLS)

    xf = x.reshape(m, D_MODEL)
    h = _norm(xf, norm1_w[0].reshape(1, D_MODEL), tl["norm"])
    for l in range(depth):
        w = w_in[l]
        w_main = jnp.concatenate(
            [w[:, :o_glr], w[:, o_z:o_dt], w[:, o_ga:]], axis=1).astype(BF16)
        w_small = jnp.concatenate(
            [w[:, o_glr:o_z], w[:, o_dt:o_ga],
             jnp.zeros((D_MODEL, SMALL_COLS - GLA_RANK - SSM_HEADS), F32)], axis=1).astype(BF16)
        u, small = _inproj(h, w_main, w_small, tl["inproj_m"], tl["inproj_n"])

        w2_hi, w2_lo = _split_hi_lo(
            jnp.pad(gla_gate_w2[l], ((0, SMALL_COLS - GLA_RANK), (0, 0))))
        oa = _gla(u, small, jnp.concatenate([w2_hi, w2_hi, w2_lo], axis=0),
                  gla_gate_b[l].reshape(1, GLA_KEY),
                  gla_norm_w[l].reshape(1, GLA_DV), gla_mat, batch, seq, tl["gla"])

        ob = _ssd(u, small, _ssd_group_cols(ssm_conv_w[l]),
                  _ssd_group_cols(ssm_conv_b[l].reshape(1, SSM_XBC)),
                  small_row(ssm_dt_bias[l]), small_row(-jnp.exp(ssm_A_log[l])),
                  jnp.repeat(ssm_D[l], SSM_P).reshape(1, SSM_INNER),
                  ssm_norm_w[l].reshape(1, SSM_INNER),
                  ssd_ex, ssd_tri2, ssd_shift, batch, seq, tl["ssd"])

        xf, h2 = _merge(oa, ob, u, xf, w_branch_a[l].astype(BF16), w_branch_b[l].astype(BF16),
                        w_mix_out[l].astype(BF16), norm2_w[l].reshape(1, D_MODEL), tl["merge"])

        wf = w_ffn_in[l]
        wg, wu = wf[:, :FFN_HIDDEN].astype(BF16), wf[:, FFN_HIDDEN:].astype(BF16)
        wo = w_ffn_out[l].astype(BF16)
        if l + 1 < depth:
            xf, h = _ffn(h2, xf, wg, wu, wo, norm1_w[l + 1].reshape(1, D_MODEL), tl["ffn"], False)
        else:
            xf = _ffn(h2, xf, wg, wu, wo, final_norm_w.reshape(1, D_MODEL), tl["ffn"], True)
    return xf.reshape(batch, seq, D_MODEL)
```

```python
import functools

import numpy as np
import jax
import jax.numpy as jnp
from jax import lax
from jax.experimental import pallas as pl
from jax.experimental.pallas import tpu as pltpu

F32 = jnp.float32
BF16 = jnp.bfloat16

D_MODEL = 1024
EPS = 1e-6
GLA_HEADS = 4
GLA_DK = 128
GLA_DV = 256
GLA_KEY = GLA_HEADS * GLA_DK
GLA_VAL = GLA_HEADS * GLA_DV
GLA_RANK = 16
GLA_TAU = 16.0
GLA_CHUNK = 64
GLA_LEVELS = 6
GLA_MXU_LEVELS = 2
SSM_INNER = 2048
SSM_P = 64
SSM_HEADS = 32
SSM_GROUPS = 4
SSM_N = 128
SSM_CONV = 4
SSM_CHUNK = 128
SSM_HPG = SSM_HEADS // SSM_GROUPS
SSM_GW = SSM_HPG * SSM_P
SSM_CW = SSM_GW + 2 * SSM_N
SSM_XBC = SSM_INNER + 2 * SSM_GROUPS * SSM_N
FFN_HIDDEN = 2816
IN_SPLITS = (GLA_KEY, GLA_KEY, GLA_VAL, GLA_VAL, GLA_RANK, SSM_INNER, SSM_XBC,
             SSM_HEADS, D_MODEL, D_MODEL)

OFF_Q = 0
OFF_K = OFF_Q + GLA_KEY
OFF_V = OFF_K + GLA_KEY
OFF_R = OFF_V + GLA_VAL
OFF_Z = OFF_R + GLA_VAL
OFF_X = OFF_Z + SSM_INNER
OFF_B = OFF_X + SSM_INNER
OFF_C = OFF_B + SSM_GROUPS * SSM_N
OFF_GA = OFF_C + SSM_GROUPS * SSM_N
OFF_GB = OFF_GA + D_MODEL
U_COLS = OFF_GB + D_MODEL
SMALL_COLS = 128
DT_LANE0 = GLA_RANK
HIST_ROWS = 16

LOG2E = 1.4426950408889634

VMEM_LIMIT = 48 * 1024 * 1024


def _silu_half(hx):
    return hx * jnp.tanh(hx) + hx


def _sigmoid2_half(hx):
    return jnp.tanh(hx) + 1.0


def _softplus(x):
    return jnp.maximum(x, 0.0) + jnp.log1p(jnp.exp(-jnp.abs(x)))


def _log_sigmoid(x):
    return jnp.minimum(x, 0.0) - jnp.log1p(jnp.exp(-jnp.abs(x)))


def _split_hi_lo(x):
    hi = x.astype(BF16)
    lo = (x - hi.astype(F32)).astype(BF16)
    return hi, lo


def _hi_lo(x):
    hi, lo = _split_hi_lo(x)
    return jnp.concatenate([hi, lo], axis=1)


def _dot(a, b):
    return jnp.dot(a, b, preferred_element_type=F32)


def _dot_nt(a, b):
    return lax.dot_general(a, b, (((1,), (1,)), ((), ())), preferred_element_type=F32)


def _dot_tn(a, b):
    return lax.dot_general(a, b, (((0,), (0,)), ((), ())), preferred_element_type=F32)


def _params(semantics):
    return pltpu.CompilerParams(dimension_semantics=semantics, vmem_limit_bytes=VMEM_LIMIT)


def _rms_rows(x, w):
    return x * lax.rsqrt(jnp.mean(x * x, axis=-1, keepdims=True) + EPS) * w


def _norm_kernel(x_ref, w_ref, h_ref):
    h_ref[...] = _rms_rows(x_ref[...], w_ref[...]).astype(BF16)


def _norm(x, w, tm):
    m = x.shape[0]
    return pl.pallas_call(
        _norm_kernel,
        grid=(m // tm,),
        in_specs=[pl.BlockSpec((tm, D_MODEL), lambda i: (i, 0)),
                  pl.BlockSpec((1, D_MODEL), lambda i: (0, 0))],
        out_specs=pl.BlockSpec((tm, D_MODEL), lambda i: (i, 0)),
        out_shape=jax.ShapeDtypeStruct((m, D_MODEL), BF16),
        compiler_params=_params(("parallel",)),
        name="rms_norm",
    )(x, w)


def _inproj_kernel(h_ref, w_ref, ws_ref, u_ref, s_ref):
    h = h_ref[...]
    u_ref[...] = _dot(h, w_ref[...]).astype(BF16)

    @pl.when(pl.program_id(1) == 0)
    def _():
        s_ref[...] = _dot(h, ws_ref[...])


def _inproj(h, w_main, w_small, tm, tn):
    m = h.shape[0]
    return pl.pallas_call(
        _inproj_kernel,
        grid=(m // tm, U_COLS // tn),
        in_specs=[pl.BlockSpec((tm, D_MODEL), lambda i, j: (i, 0)),
                  pl.BlockSpec((D_MODEL, tn), lambda i, j: (0, j)),
                  pl.BlockSpec((D_MODEL, SMALL_COLS), lambda i, j: (0, 0))],
        out_specs=[pl.BlockSpec((tm, tn), lambda i, j: (i, j)),
                   pl.BlockSpec((tm, SMALL_COLS), lambda i, j: (i, 0))],
        out_shape=[jax.ShapeDtypeStruct((m, U_COLS), BF16),
                   jax.ShapeDtypeStruct((m, SMALL_COLS), F32)],
        compiler_params=_params(("parallel", "arbitrary")),
        name="in_proj",
    )(h, w_main, w_small)


def _gla_constants():
    c = GLA_CHUNK
    mat = np.zeros((1 + GLA_MXU_LEVELS, c, c), np.float32)
    t = np.arange(c)[:, None]
    tau = np.arange(c)[None, :]
    mat[0] = tau <= t
    for lvl in range(GLA_MXU_LEVELS):
        s = 1 << lvl
        mid = (t // (2 * s)) * (2 * s) + s
        upper = t >= mid
        mat[1 + lvl] = np.where(upper, (tau >= mid) & (tau <= t), (tau > t) & (tau <= mid - 1))
    return jnp.asarray(mat.reshape((1 + GLA_MXU_LEVELS) * c, c), BF16)


def _gla_level_mask(lvl):
    c = GLA_CHUNK
    s = 1 << lvl
    i = lax.broadcasted_iota(jnp.int32, (c, c), 0)
    j = lax.broadcasted_iota(jnp.int32, (c, c), 1)
    same = (i // (2 * s)) == (j // (2 * s))
    return same & ((i % (2 * s)) >= s) & ((j % (2 * s)) < s)


def _gla_kernel(q_ref, k_ref, v_ref, r_ref, s_ref, w2_ref, b2_ref, nw_ref, mat_ref,
                o_ref, st_ref, *, n_chunks):
    c = GLA_CHUNK

    @pl.when(pl.program_id(1) == 0)
    def _():
        st_ref[...] = jnp.zeros_like(st_ref)

    w2 = w2_ref[...]
    b2 = b2_ref[...]
    nw = nw_ref[...]
    mat = mat_ref[...]
    ii = lax.broadcasted_iota(jnp.int32, (c, c), 0)
    jj = lax.broadcasted_iota(jnp.int32, (c, c), 1)
    masks = [_gla_level_mask(lvl) for lvl in range(GLA_LEVELS)]
    eye = ii == jj
    heads = range(GLA_HEADS)
    kcs = [slice(h * GLA_DK, (h + 1) * GLA_DK) for h in heads]
    vcs = [slice(h * GLA_DV, (h + 1) * GLA_DV) for h in heads]

    def ref_rows(b, s):
        blk = b.reshape(c // (2 * s), 2 * s, GLA_DK)
        return jnp.broadcast_to(blk[:, s - 1:s, :], blk.shape).reshape(c, GLA_DK)

    def chunk(ci, carry):
        r0 = pl.multiple_of(ci * c, c)
        rows = pl.ds(r0, c)
        l_hi, l_lo = _split_hi_lo(s_ref[rows, :])
        logit = _dot(jnp.concatenate([l_hi, l_lo, l_hi], axis=1), w2) + b2
        g_all = _log_sigmoid(logit) * (LOG2E / GLA_TAU)

        x_inter, x_state, x_lvl, d_col = [], [], [], []
        for h in heads:
            e2 = _dot(mat, _hi_lo(g_all[:, kcs[h]]))
            e = e2[:, :GLA_DK] + e2[:, GLA_DK:]
            b = e[0:c]
            x_inter.append(jnp.exp2(b))
            x_state.append(jnp.exp2(b[c - 1:c] - b))
            lv = [jnp.exp2(e[(1 + l) * c:(2 + l) * c]) for l in range(GLA_MXU_LEVELS)]
            lv += [jnp.exp2(-jnp.abs(b - ref_rows(b, 1 << l)))
                   for l in range(GLA_MXU_LEVELS, GLA_LEVELS)]
            x_lvl.append(lv)
            d_col.append(jnp.exp2(b.T[:, c - 1:c]))

        amat = []
        for h in heads:
            q = q_ref[rows, kcs[h]].astype(F32) * (GLA_DK ** -0.5)
            k = k_ref[rows, kcs[h]].astype(F32)
            a = jnp.where(eye, _dot_nt(q.astype(BF16), k.astype(BF16)), 0.0)
            for lvl in range(GLA_LEVELS):
                xl = x_lvl[h][lvl]
                p = _dot_nt((q * xl).astype(BF16), (k * xl).astype(BF16))
                a = a + jnp.where(masks[lvl], p, 0.0)
            amat.append(a.astype(BF16))

        outs = []
        for h in heads:
            q = q_ref[rows, kcs[h]].astype(F32) * (GLA_DK ** -0.5)
            k = k_ref[rows, kcs[h]].astype(F32)
            v = v_ref[rows, vcs[h]]
            st = st_ref[h]
            outs.append(_dot((q * x_inter[h]).astype(BF16), st.astype(BF16)) + _dot(amat[h], v))
            ks = (k * x_state[h]).astype(BF16)
            st_ref[h] = st * d_col[h] + _dot_tn(ks, v)

        for h in heads:
            o = outs[h]
            rr = r_ref[rows, vcs[h]].astype(F32)
            on = o * lax.rsqrt(jnp.mean(o * o, axis=-1, keepdims=True) + EPS) * nw
            o_ref[rows, vcs[h]] = (on * _silu_half(rr)).astype(BF16)
        return carry

    lax.fori_loop(0, n_chunks, chunk, 0, unroll=2)


def _gla(u, small, w2, b2, nw, mat, batch, seq, tb):
    m = batch * seq
    nt = seq // tb
    row = lambda b, t: b * nt + t
    kern = functools.partial(_gla_kernel, n_chunks=tb // GLA_CHUNK)
    return pl.pallas_call(
        kern,
        grid=(batch, nt),
        in_specs=[
            pl.BlockSpec((tb, GLA_KEY), lambda b, t: (row(b, t), OFF_Q // GLA_KEY)),
            pl.BlockSpec((tb, GLA_KEY), lambda b, t: (row(b, t), OFF_K // GLA_KEY)),
            pl.BlockSpec((tb, GLA_VAL), lambda b, t: (row(b, t), OFF_V // GLA_VAL)),
            pl.BlockSpec((tb, GLA_VAL), lambda b, t: (row(b, t), OFF_R // GLA_VAL)),
            pl.BlockSpec((tb, SMALL_COLS), lambda b, t: (row(b, t), 0)),
            pl.BlockSpec((3 * SMALL_COLS, GLA_KEY), lambda b, t: (0, 0)),
            pl.BlockSpec((1, GLA_KEY), lambda b, t: (0, 0)),
            pl.BlockSpec((1, GLA_DV), lambda b, t: (0, 0)),
            pl.BlockSpec(((1 + GLA_MXU_LEVELS) * GLA_CHUNK, GLA_CHUNK), lambda b, t: (0, 0)),
        ],
        out_specs=pl.BlockSpec((tb, GLA_VAL), lambda b, t: (row(b, t), 0)),
        out_shape=jax.ShapeDtypeStruct((m, GLA_VAL), BF16),
        scratch_shapes=[pltpu.VMEM((GLA_HEADS, GLA_DK, GLA_DV), F32)],
        compiler_params=_params(("parallel", "arbitrary")),
        name="gla_mixer",
    )(u, u, u, u, small, w2, b2, nw, mat)


def _ssd_constants():
    ex = np.zeros((SSM_GROUPS, SMALL_COLS, SSM_GW), np.float32)
    for g in range(SSM_GROUPS):
        for j in range(SSM_HPG):
            ex[g, DT_LANE0 + g * SSM_HPG + j, j * SSM_P:(j + 1) * SSM_P] = 1.0
    ln = SSM_CHUNK
    tril = np.tril(np.ones((ln, ln), np.float32))
    tri2 = np.concatenate([tril, 1.0 - tril], axis=0)
    shift = np.zeros((SSM_CONV - 1, ln, ln + HIST_ROWS), np.float32)
    for k in range(SSM_CONV - 1):
        shift[k, np.arange(ln), HIST_ROWS + np.arange(ln) - (SSM_CONV - 1) + k] = 1.0
    return (jnp.asarray(ex, BF16), jnp.asarray(tri2, BF16),
            jnp.asarray(shift.reshape((SSM_CONV - 1) * ln, ln + HIST_ROWS), BF16))


def _ssd_kernel(z_ref, x_ref, bm_ref, cm_ref, xh_ref, bh_ref, ch_ref, s_ref, cw_ref, cb_ref,
                dtb_ref, arow_ref, dexp_ref, nw_ref, ex_ref, tri2_ref, shift_ref,
                o_ref, st_ref, *, n_chunks, tb):
    ln = SSM_CHUNK
    hr = HIST_ROWS
    gw = SSM_GW
    first = pl.program_id(2) == 0

    @pl.when(first)
    def _():
        st_ref[...] = jnp.zeros_like(st_ref)

    def raw_rows(lo, hi):
        return jnp.concatenate([x_ref[lo:hi, :], bm_ref[lo:hi, :], cm_ref[lo:hi, :]], axis=1)

    hist = jnp.concatenate([xh_ref[...], bh_ref[...], ch_ref[...]], axis=1)
    hist = jnp.where(first, jnp.zeros_like(hist), hist)

    tri2 = tri2_ref[...]
    shift = shift_ref[...]
    exm = ex_ref[...]
    cw = cw_ref[...]
    cbias = cb_ref[...]
    dtb = dtb_ref[...]
    a_row = arow_ref[...]
    d_exp = dexp_ref[...]
    nw = nw_ref[...]
    ii = lax.broadcasted_iota(jnp.int32, (ln, ln), 0)
    jj = lax.broadcasted_iota(jnp.int32, (ln, ln), 1)
    causal = jj <= ii
    low_half = lax.broadcasted_iota(jnp.int32, (ln, 2 * SSM_P), 1) < SSM_P
    loc_shift = SMALL_COLS - DT_LANE0 - pl.program_id(1) * SSM_HPG

    def prep(ci):
        r0 = ci * ln
        if ci == 0:
            ext = jnp.concatenate([hist, raw_rows(0, ln)], axis=0)
        else:
            ext = raw_rows(r0 - hr, r0 + ln)
        taps = _dot(shift, ext)
        acc = cbias + ext[hr:, :].astype(F32) * cw[SSM_CONV - 1:SSM_CONV, :]
        for k in range(SSM_CONV - 1):
            acc = acc + taps[k * ln:(k + 1) * ln] * cw[k:k + 1, :]
        xbc = _silu_half(acc)
        xc = xbc[:, :gw]
        bc = xbc[:, gw:gw + SSM_N].astype(BF16)
        cc = xbc[:, gw + SSM_N:].astype(BF16)

        dt_small = _softplus(s_ref[r0:r0 + ln, :] + dtb)
        cs = _dot(tri2, _hi_lo(dt_small * a_row))
        cum_small = cs[:ln, :SMALL_COLS] + cs[:ln, SMALL_COLS:]
        rcum_small = cs[ln:, :SMALL_COLS] + cs[ln:, SMALL_COLS:]
        small3 = jnp.concatenate([dt_small.astype(BF16), jnp.exp2(cum_small).astype(BF16),
                                  jnp.exp2(rcum_small).astype(BF16)], axis=0)
        wide3 = _dot(small3, exm)
        cum_loc = pltpu.roll(cum_small, loc_shift, axis=1)
        cum_t = cum_loc.T
        return xc, bc, cc, wide3[:ln], wide3[ln:2 * ln], wide3[2 * ln:], cum_loc, cum_t

    def finish(ci, front):
        r0 = ci * ln
        xc, bc, cc, dt_exp, ecum, edst, cum_loc, cum_t = front
        xdt = xc * dt_exp
        xdt_b = xdt.astype(BF16)
        st = st_ref[...]
        y_off = _dot(cc, st.astype(BF16)) * ecum
        cbm = jnp.where(causal, _dot_nt(cc, bc), 0.0)
        yd = []
        for p in range(SSM_HPG // 2):
            ws, xm = [], []
            xp = xdt_b[:, 2 * p * SSM_P:(2 * p + 2) * SSM_P]
            for j, keep in ((2 * p, low_half), (2 * p + 1, ~low_half)):
                seg = cum_loc[:, j:j + 1] - cum_t[j:j + 1, :]
                ws.append((cbm * jnp.exp2(jnp.minimum(seg, 0.0))).astype(BF16))
                xm.append(jnp.where(keep, xp, jnp.zeros_like(xp)))
            yd.append(_dot(jnp.concatenate(ws, axis=1), jnp.concatenate(xm, axis=0)))
        y = y_off + jnp.concatenate(yd, axis=1)

        xd = (xdt * edst).astype(BF16)
        st_ref[...] = st * ecum[ln - 1:ln, :] + _dot_tn(bc, xd)

        y = y + xc * d_exp
        y = y * _silu_half(z_ref[r0:r0 + ln, :].astype(F32))
        y = y * lax.rsqrt(jnp.mean(y * y, axis=-1, keepdims=True) + EPS) * nw
        o_ref[r0:r0 + ln, :] = y.astype(BF16)

    front = prep(0)
    for ci in range(n_chunks):
        nxt = prep(ci + 1) if ci + 1 < n_chunks else None
        finish(ci, front)
        front = nxt


def _ssd(u, small, cw, cb, dtb_row, a_row, d_exp, nw, exm, tri2, shift, batch, seq, tb):
    m = batch * seq
    nt = seq // tb
    row = lambda b, g, t: b * nt + t
    gcol = lambda off, width: (lambda b, g, t: (row(b, g, t), off // width + g))
    hist_row = lambda b, g, t: jnp.maximum(row(b, g, t) * (tb // HIST_ROWS) - 1, 0)
    hcol = lambda off, width: (lambda b, g, t: (hist_row(b, g, t), off // width + g))
    const2 = lambda shape: pl.BlockSpec(shape, lambda b, g, t: (0, 0))
    per_group = lambda r, c: pl.BlockSpec((None, r, c), lambda b, g, t: (g, 0, 0))
    kern = functools.partial(_ssd_kernel, n_chunks=tb // SSM_CHUNK, tb=tb)
    return pl.pallas_call(
        kern,
        grid=(batch, SSM_GROUPS, nt),
        in_specs=[
            pl.BlockSpec((tb, SSM_GW), gcol(OFF_Z, SSM_GW)),
            pl.BlockSpec((tb, SSM_GW), gcol(OFF_X, SSM_GW)),
            pl.BlockSpec((tb, SSM_N), gcol(OFF_B, SSM_N)),
            pl.BlockSpec((tb, SSM_N), gcol(OFF_C, SSM_N)),
            pl.BlockSpec((HIST_ROWS, SSM_GW), hcol(OFF_X, SSM_GW)),
            pl.BlockSpec((HIST_ROWS, SSM_N), hcol(OFF_B, SSM_N)),
            pl.BlockSpec((HIST_ROWS, SSM_N), hcol(OFF_C, SSM_N)),
            pl.BlockSpec((tb, SMALL_COLS), lambda b, g, t: (row(b, g, t), 0)),
            per_group(SSM_CONV, SSM_CW),
            per_group(1, SSM_CW),
            const2((1, SMALL_COLS)),
            const2((1, SMALL_COLS)),
            pl.BlockSpec((1, SSM_GW), lambda b, g, t: (0, g)),
            pl.BlockSpec((1, SSM_GW), lambda b, g, t: (0, g)),
            per_group(SMALL_COLS, SSM_GW),
            const2((2 * SSM_CHUNK, SSM_CHUNK)),
            const2(((SSM_CONV - 1) * SSM_CHUNK, SSM_CHUNK + HIST_ROWS)),
        ],
        out_specs=pl.BlockSpec((tb, SSM_GW), lambda b, g, t: (row(b, g, t), g)),
        out_shape=jax.ShapeDtypeStruct((m, SSM_INNER), BF16),
        scratch_shapes=[pltpu.VMEM((SSM_N, SSM_GW), F32)],
        compiler_params=_params(("parallel", "parallel", "arbitrary")),
        name="ssd_mixer",
    )(u, u, u, u, u, u, u, small, cw, cb, dtb_row, a_row, d_exp, nw, exm, tri2, shift)


def _merge_kernel(oa_ref, ob_ref, ga_ref, gb_ref, x_ref, wa_ref, wb_ref, wo_ref, nw_ref,
                  xo_ref, h_ref):
    ya = _dot(oa_ref[...], wa_ref[...])
    yb = _dot(ob_ref[...], wb_ref[...])
    mix = (_sigmoid2_half(ga_ref[...].astype(F32)) * ya
           + _sigmoid2_half(gb_ref[...].astype(F32)) * yb)
    xn = x_ref[...] + _dot(mix.astype(BF16), wo_ref[...])
    xo_ref[...] = xn
    h_ref[...] = _rms_rows(xn, nw_ref[...]).astype(BF16)


def _merge(oa, ob, u, x, wa, wb, wo, nw, tm):
    m = x.shape[0]
    full = lambda shape: pl.BlockSpec(shape, lambda i: (0, 0))
    return pl.pallas_call(
        _merge_kernel,
        grid=(m // tm,),
        in_specs=[pl.BlockSpec((tm, GLA_VAL), lambda i: (i, 0)),
                  pl.BlockSpec((tm, SSM_INNER), lambda i: (i, 0)),
                  pl.BlockSpec((tm, D_MODEL), lambda i: (i, OFF_GA // D_MODEL)),
                  pl.BlockSpec((tm, D_MODEL), lambda i: (i, OFF_GB // D_MODEL)),
                  pl.BlockSpec((tm, D_MODEL), lambda i: (i, 0)),
                  full((GLA_VAL, D_MODEL)), full((SSM_INNER, D_MODEL)),
                  full((D_MODEL, D_MODEL)), full((1, D_MODEL))],
        out_specs=[pl.BlockSpec((tm, D_MODEL), lambda i: (i, 0)),
                   pl.BlockSpec((tm, D_MODEL), lambda i: (i, 0))],
        out_shape=[jax.ShapeDtypeStruct((m, D_MODEL), F32),
                   jax.ShapeDtypeStruct((m, D_MODEL), BF16)],
        compiler_params=_params(("parallel",)),
        name="merge_out",
    )(oa, ob, u, u, x, wa, wb, wo, nw)


def _ffn_kernel(h_ref, x_ref, wg_ref, wu_ref, wo_ref, nw_ref, o_ref, *, last):
    h = h_ref[...]
    gate = _dot(h, wg_ref[...])
    up = _dot(h, wu_ref[...])
    act = (_silu_half(gate) * up).astype(BF16)
    xn = x_ref[...] + _dot(act, wo_ref[...])
    if last:
        o_ref[...] = _rms_rows(xn, nw_ref[...])
    else:
        o_ref[0][...] = xn
        o_ref[1][...] = _rms_rows(xn, nw_ref[...]).astype(BF16)


def _ffn_kernel_mid(h_ref, x_ref, wg_ref, wu_ref, wo_ref, nw_ref, xo_ref, ho_ref):
    _ffn_kernel(h_ref, x_ref, wg_ref, wu_ref, wo_ref, nw_ref, (xo_ref, ho_ref), last=False)


def _ffn_kernel_last(h_ref, x_ref, wg_ref, wu_ref, wo_ref, nw_ref, o_ref):
    _ffn_kernel(h_ref, x_ref, wg_ref, wu_ref, wo_ref, nw_ref, o_ref, last=True)


def _ffn(h, x, wg, wu, wo, nw, tm, last):
    m = x.shape[0]
    full = lambda shape: pl.BlockSpec(shape, lambda i: (0, 0))
    rows = pl.BlockSpec((tm, D_MODEL), lambda i: (i, 0))
    if last:
        kern, out_specs = _ffn_kernel_last, rows
        out_shape = jax.ShapeDtypeStruct((m, D_MODEL), F32)
    else:
        kern, out_specs = _ffn_kernel_mid, [rows, rows]
        out_shape = [jax.ShapeDtypeStruct((m, D_MODEL), F32),
                     jax.ShapeDtypeStruct((m, D_MODEL), BF16)]
    return pl.pallas_call(
        kern,
        grid=(m // tm,),
        in_specs=[rows, rows, full((D_MODEL, FFN_HIDDEN)), full((D_MODEL, FFN_HIDDEN)),
                  full((FFN_HIDDEN, D_MODEL)), full((1, D_MODEL))],
        out_specs=out_specs,
        out_shape=out_shape,
        compiler_params=_params(("parallel",)),
        name="ffn_last" if last else "ffn",
    )(h, x, wg, wu, wo, nw)


def _tiles(batch, seq):
    m = batch * seq
    pick = lambda n, cands: next(c for c in cands if n % c == 0)
    return dict(
        norm=pick(m, (1024, 512, 256, 128)),
        inproj_m=pick(m, (1024, 512, 256, 128)),
        inproj_n=1024,
        gla=pick(seq, (1024, 512, 256, 128, 64)),
        ssd=pick(seq, (1024, 512, 256, 128)),
        merge=pick(m, (256, 128)),
        ffn=pick(m, (256, 128)),
    )


def _ssd_group_cols(a):
    parts = []
    for g in range(SSM_GROUPS):
        b0 = SSM_INNER + g * SSM_N
        c0 = SSM_INNER + SSM_GROUPS * SSM_N + g * SSM_N
        parts.append(jnp.concatenate(
            [a[..., g * SSM_GW:(g + 1) * SSM_GW], a[..., b0:b0 + SSM_N], a[..., c0:c0 + SSM_N]],
            axis=-1))
    return jnp.stack(parts, axis=1)


def kernel(x, norm1_w, w_in, gla_gate_w2, gla_gate_b, gla_norm_w, ssm_conv_w, ssm_conv_b,
           ssm_dt_bias, ssm_A_log, ssm_D, ssm_norm_w, w_branch_a, w_branch_b, w_mix_out,
           norm2_w, w_ffn_in, w_ffn_out, final_norm_w):
    batch, seq, _ = x.shape
    depth = w_in.shape[0]
    m = batch * seq
    tl = _tiles(batch, seq)
    gla_mat = _gla_constants()
    ssd_ex, ssd_tri2, ssd_shift = _ssd_constants()

    offs = np.cumsum((0,) + IN_SPLITS)
    o_glr, o_z, o_dt, o_ga = offs[4], offs[5], offs[7], offs[8]

    def small_rows(v):
        return jnp.pad(v, ((0, 0), (DT_LANE0, SMALL_COLS - DT_LANE0 - SSM_HEADS)))[:, None, :]

    w2_hi, w2_lo = _split_hi_lo(
        jnp.pad(gla_gate_w2, ((0, 0), (0, SMALL_COLS - GLA_RANK), (0, 0))))
    w2_all = jnp.concatenate([w2_hi, w2_hi, w2_lo], axis=1)
    conv_w_all = _ssd_group_cols(0.5 * ssm_conv_w)
    conv_b_all = _ssd_group_cols(0.5 * ssm_conv_b[:, None, :])
    dt_bias_all = small_rows(ssm_dt_bias)
    a_all = small_rows(-LOG2E * jnp.exp(ssm_A_log))
    d_all = jnp.repeat(ssm_D, SSM_P, axis=1)[:, None, :]

    gate_half = np.ones((1, U_COLS), np.float32)
    gate_half[:, OFF_R:OFF_X] = 0.5
    gate_half[:, OFF_GA:] = 0.5

    xf = x.reshape(m, D_MODEL)
    h = _norm(xf, norm1_w[0].reshape(1, D_MODEL), tl["norm"])
    for l in range(depth):
        w = w_in[l]
        w_main = (jnp.concatenate([w[:, :o_glr], w[:, o_z:o_dt], w[:, o_ga:]], axis=1)
                  * gate_half).astype(BF16)
        w_small = jnp.concatenate(
            [w[:, o_glr:o_z], w[:, o_dt:o_ga],
             jnp.zeros((D_MODEL, SMALL_COLS - GLA_RANK - SSM_HEADS), F32)], axis=1).astype(BF16)
        u, small = _inproj(h, w_main, w_small, tl["inproj_m"], tl["inproj_n"])

        oa = _gla(u, small, w2_all[l], gla_gate_b[l].reshape(1, GLA_KEY),
                  gla_norm_w[l].reshape(1, GLA_DV), gla_mat, batch, seq, tl["gla"])

        ob = _ssd(u, small, conv_w_all[l], conv_b_all[l], dt_bias_all[l], a_all[l], d_all[l],
                  ssm_norm_w[l].reshape(1, SSM_INNER),
                  ssd_ex, ssd_tri2, ssd_shift, batch, seq, tl["ssd"])

        xf, h2 = _merge(oa, ob, u, xf, w_branch_a[l].astype(BF16), w_branch_b[l].astype(BF16),
                        (0.5 * w_mix_out[l]).astype(BF16), norm2_w[l].reshape(1, D_MODEL),
                        tl["merge"])

        wf = w_ffn_in[l]
        wg, wu = (0.5 * wf[:, :FFN_HIDDEN]).astype(BF16), wf[:, FFN_HIDDEN:].astype(BF16)
        wo = w_ffn_out[l].astype(BF16)
        if l + 1 < depth:
            xf, h = _ffn(h2, xf, wg, wu, wo, norm1_w[l + 1].reshape(1, D_MODEL), tl["ffn"], False)
        else:
            xf = _ffn(h2, xf, wg, wu, wo, final_norm_w.reshape(1, D_MODEL), tl["ffn"], True)
    return xf.reshape(batch, seq, D_MODEL)
```

```python
import functools

import numpy as np
import jax
import jax.numpy as jnp
from jax import lax
from jax.experimental import pallas as pl
from jax.experimental.pallas import tpu as pltpu

F32 = jnp.float32
BF16 = jnp.bfloat16

D_MODEL = 1024
EPS = 1e-6
GLA_HEADS = 4
GLA_DK = 128
GLA_DV = 256
GLA_KEY = GLA_HEADS * GLA_DK
GLA_VAL = GLA_HEADS * GLA_DV
GLA_RANK = 16
GLA_TAU = 16.0
GLA_CHUNK = 64
GLA_LEVELS = 6
GLA_MXU_LEVELS = 2
SSM_INNER = 2048
SSM_P = 64
SSM_HEADS = 32
SSM_GROUPS = 4
SSM_N = 128
SSM_CONV = 4
SSM_CHUNK = 128
SSM_HPG = SSM_HEADS // SSM_GROUPS
SSM_GW = SSM_HPG * SSM_P
SSM_CW = SSM_GW + 2 * SSM_N
SSM_XBC = SSM_INNER + 2 * SSM_GROUPS * SSM_N
FFN_HIDDEN = 2816
IN_SPLITS = (GLA_KEY, GLA_KEY, GLA_VAL, GLA_VAL, GLA_RANK, SSM_INNER, SSM_XBC,
             SSM_HEADS, D_MODEL, D_MODEL)

OFF_Q = 0
OFF_K = OFF_Q + GLA_KEY
OFF_V = OFF_K + GLA_KEY
OFF_Z = OFF_V + GLA_VAL
OFF_R = OFF_Z + SSM_INNER
OFF_X = OFF_R + GLA_VAL
OFF_B = OFF_X + SSM_INNER
OFF_C = OFF_B + SSM_GROUPS * SSM_N
OFF_GA = OFF_C + SSM_GROUPS * SSM_N
OFF_GB = OFF_GA + D_MODEL
U_COLS = OFF_GB + D_MODEL
SMALL_COLS = 128
DT_LANE0 = GLA_RANK
HIST_ROWS = 16

LOG2E = 1.4426950408889634

VMEM_LIMIT = 48 * 1024 * 1024


def _silu_half(hx):
    return hx * jnp.tanh(hx) + hx


def _sigmoid2_half(hx):
    return jnp.tanh(hx) + 1.0


def _softplus(x):
    return jnp.maximum(x, 0.0) + jnp.log1p(jnp.exp(-jnp.abs(x)))


def _log_sigmoid(x):
    return jnp.minimum(x, 0.0) - jnp.log1p(jnp.exp(-jnp.abs(x)))


def _split_hi_lo(x):
    hi = x.astype(BF16)
    lo = (x - hi.astype(F32)).astype(BF16)
    return hi, lo


def _hi_lo(x):
    hi, lo = _split_hi_lo(x)
    return jnp.concatenate([hi, lo], axis=1)


def _dot(a, b):
    return jnp.dot(a, b, preferred_element_type=F32)


def _dot_nt(a, b):
    return lax.dot_general(a, b, (((1,), (1,)), ((), ())), preferred_element_type=F32)


def _dot_tn(a, b):
    return lax.dot_general(a, b, (((0,), (0,)), ((), ())), preferred_element_type=F32)


def _params(semantics):
    return pltpu.CompilerParams(dimension_semantics=semantics, vmem_limit_bytes=VMEM_LIMIT)


def _rms_rows(x, w):
    return x * lax.rsqrt(jnp.mean(x * x, axis=-1, keepdims=True) + EPS) * w


def _norm_kernel(x_ref, w_ref, h_ref):
    h_ref[...] = _rms_rows(x_ref[...], w_ref[...]).astype(BF16)


def _norm(x, w, tm):
    m = x.shape[0]
    return pl.pallas_call(
        _norm_kernel,
        grid=(m // tm,),
        in_specs=[pl.BlockSpec((tm, D_MODEL), lambda i: (i, 0)),
                  pl.BlockSpec((1, D_MODEL), lambda i: (0, 0))],
        out_specs=pl.BlockSpec((tm, D_MODEL), lambda i: (i, 0)),
        out_shape=jax.ShapeDtypeStruct((m, D_MODEL), BF16),
        compiler_params=_params(("parallel",)),
        name="rms_norm",
    )(x, w)


def _inproj_kernel(h_ref, w_ref, ws_ref, u_ref, s_ref):
    h = h_ref[...]
    u_ref[...] = _dot(h, w_ref[...]).astype(BF16)

    @pl.when(pl.program_id(1) == 0)
    def _():
        s_ref[...] = _dot(h, ws_ref[...])


def _inproj(h, w_main, w_small, tm, tn):
    m = h.shape[0]
    return pl.pallas_call(
        _inproj_kernel,
        grid=(m // tm, U_COLS // tn),
        in_specs=[pl.BlockSpec((tm, D_MODEL), lambda i, j: (i, 0)),
                  pl.BlockSpec((D_MODEL, tn), lambda i, j: (0, j)),
                  pl.BlockSpec((D_MODEL, SMALL_COLS), lambda i, j: (0, 0))],
        out_specs=[pl.BlockSpec((tm, tn), lambda i, j: (i, j)),
                   pl.BlockSpec((tm, SMALL_COLS), lambda i, j: (i, 0))],
        out_shape=[jax.ShapeDtypeStruct((m, U_COLS), BF16),
                   jax.ShapeDtypeStruct((m, SMALL_COLS), F32)],
        compiler_params=_params(("parallel", "arbitrary")),
        name="in_proj",
    )(h, w_main, w_small)


def _gla_constants():
    c = GLA_CHUNK
    mat = np.zeros((1 + GLA_MXU_LEVELS, c, c), np.float32)
    t = np.arange(c)[:, None]
    tau = np.arange(c)[None, :]
    mat[0] = tau <= t
    for lvl in range(GLA_MXU_LEVELS):
        s = 1 << lvl
        mid = (t // (2 * s)) * (2 * s) + s
        upper = t >= mid
        mat[1 + lvl] = np.where(upper, (tau >= mid) & (tau <= t), (tau > t) & (tau <= mid - 1))
    return jnp.asarray(mat.reshape((1 + GLA_MXU_LEVELS) * c, c), BF16)


def _gla_level_mask(lvl):
    c = GLA_CHUNK
    s = 1 << lvl
    i = lax.broadcasted_iota(jnp.int32, (c, c), 0)
    j = lax.broadcasted_iota(jnp.int32, (c, c), 1)
    same = (i // (2 * s)) == (j // (2 * s))
    return same & ((i % (2 * s)) >= s) & ((j % (2 * s)) < s)


def _gla_kernel(q_ref, k_ref, v_ref, s_ref, w2_ref, b2_ref, mat_ref, o_ref, st_ref, *, n_chunks):
    c = GLA_CHUNK

    @pl.when(pl.program_id(1) == 0)
    def _():
        st_ref[...] = jnp.zeros_like(st_ref)

    w2 = w2_ref[...]
    b2 = b2_ref[...]
    mat = mat_ref[...]
    ii = lax.broadcasted_iota(jnp.int32, (c, c), 0)
    jj = lax.broadcasted_iota(jnp.int32, (c, c), 1)
    masks = [_gla_level_mask(lvl) for lvl in range(GLA_LEVELS)]
    eye = ii == jj
    heads = range(GLA_HEADS)
    kcs = [slice(h * GLA_DK, (h + 1) * GLA_DK) for h in heads]
    vcs = [slice(h * GLA_DV, (h + 1) * GLA_DV) for h in heads]

    def ref_rows(b, s):
        blk = b.reshape(c // (2 * s), 2 * s, GLA_DK)
        return jnp.broadcast_to(blk[:, s - 1:s, :], blk.shape).reshape(c, GLA_DK)

    def chunk(ci, carry):
        r0 = pl.multiple_of(ci * c, c)
        rows = pl.ds(r0, c)
        l_hi, l_lo = _split_hi_lo(s_ref[rows, :])
        logit = _dot(jnp.concatenate([l_hi, l_lo, l_hi], axis=1), w2) + b2
        g_all = _log_sigmoid(logit) * (LOG2E / GLA_TAU)

        x_inter, x_state, x_lvl, d_col = [], [], [], []
        for h in heads:
            e2 = _dot(mat, _hi_lo(g_all[:, kcs[h]]))
            e = e2[:, :GLA_DK] + e2[:, GLA_DK:]
            b = e[0:c]
            x_inter.append(jnp.exp2(b))
            x_state.append(jnp.exp2(b[c - 1:c] - b))
            lv = [jnp.exp2(e[(1 + l) * c:(2 + l) * c]) for l in range(GLA_MXU_LEVELS)]
            lv += [jnp.exp2(-jnp.abs(b - ref_rows(b, 1 << l)))
                   for l in range(GLA_MXU_LEVELS, GLA_LEVELS)]
            x_lvl.append(lv)
            d_col.append(jnp.exp2(b.T[:, c - 1:c]))

        amat = []
        for h in heads:
            q = q_ref[rows, kcs[h]].astype(F32) * (GLA_DK ** -0.5)
            k = k_ref[rows, kcs[h]].astype(F32)
            a = jnp.where(eye, _dot_nt(q.astype(BF16), k.astype(BF16)), 0.0)
            for lvl in range(GLA_LEVELS):
                xl = x_lvl[h][lvl]
                p = _dot_nt((q * xl).astype(BF16), (k * xl).astype(BF16))
                a = a + jnp.where(masks[lvl], p, 0.0)
            amat.append(a.astype(BF16))

        for h in heads:
            q = q_ref[rows, kcs[h]].astype(F32) * (GLA_DK ** -0.5)
            k = k_ref[rows, kcs[h]].astype(F32)
            v = v_ref[rows, vcs[h]]
            st = st_ref[h]
            o = _dot((q * x_inter[h]).astype(BF16), st.astype(BF16)) + _dot(amat[h], v)
            o_ref[rows, vcs[h]] = o.astype(BF16)
            ks = (k * x_state[h]).astype(BF16)
            st_ref[h] = st * d_col[h] + _dot_tn(ks, v)
        return carry

    lax.fori_loop(0, n_chunks, chunk, 0, unroll=4)


def _gla(u, small, w2, b2, mat, batch, seq, tb):
    m = batch * seq
    nt = seq // tb
    row = lambda b, t: b * nt + t
    kern = functools.partial(_gla_kernel, n_chunks=tb // GLA_CHUNK)
    return pl.pallas_call(
        kern,
        grid=(batch, nt),
        in_specs=[
            pl.BlockSpec((tb, GLA_KEY), lambda b, t: (row(b, t), OFF_Q // GLA_KEY)),
            pl.BlockSpec((tb, GLA_KEY), lambda b, t: (row(b, t), OFF_K // GLA_KEY)),
            pl.BlockSpec((tb, GLA_VAL), lambda b, t: (row(b, t), OFF_V // GLA_VAL)),
            pl.BlockSpec((tb, SMALL_COLS), lambda b, t: (row(b, t), 0)),
            pl.BlockSpec((3 * SMALL_COLS, GLA_KEY), lambda b, t: (0, 0)),
            pl.BlockSpec((1, GLA_KEY), lambda b, t: (0, 0)),
            pl.BlockSpec(((1 + GLA_MXU_LEVELS) * GLA_CHUNK, GLA_CHUNK), lambda b, t: (0, 0)),
        ],
        out_specs=pl.BlockSpec((tb, GLA_VAL), lambda b, t: (row(b, t), 0)),
        out_shape=jax.ShapeDtypeStruct((m, GLA_VAL), BF16),
        scratch_shapes=[pltpu.VMEM((GLA_HEADS, GLA_DK, GLA_DV), F32)],
        compiler_params=_params(("parallel", "arbitrary")),
        name="gla_mixer",
    )(u, u, u, small, w2, b2, mat)


def _ssd_constants():
    ex = np.zeros((SSM_GROUPS, SMALL_COLS, SSM_GW), np.float32)
    for g in range(SSM_GROUPS):
        for j in range(SSM_HPG):
            ex[g, DT_LANE0 + g * SSM_HPG + j, j * SSM_P:(j + 1) * SSM_P] = 1.0
    ln = SSM_CHUNK
    tril = np.tril(np.ones((ln, ln), np.float32))
    tri2 = np.concatenate([tril, 1.0 - tril], axis=0)
    shift = np.zeros((SSM_CONV - 1, ln, ln + HIST_ROWS), np.float32)
    for k in range(SSM_CONV - 1):
        shift[k, np.arange(ln), HIST_ROWS + np.arange(ln) - (SSM_CONV - 1) + k] = 1.0
    return (jnp.asarray(ex, BF16), jnp.asarray(tri2, BF16),
            jnp.asarray(shift.reshape((SSM_CONV - 1) * ln, ln + HIST_ROWS), BF16))


def _ssd_kernel(x_ref, bm_ref, cm_ref, xh_ref, bh_ref, ch_ref, s_ref, cw_ref, cb_ref,
                dtb_ref, arow_ref, dexp_ref, ex_ref, tri2_ref, shift_ref,
                o_ref, st_ref, *, n_chunks, tb):
    ln = SSM_CHUNK
    hr = HIST_ROWS
    gw = SSM_GW
    first = pl.program_id(2) == 0

    @pl.when(first)
    def _():
        st_ref[...] = jnp.zeros_like(st_ref)

    def raw_rows(lo, hi):
        return jnp.concatenate([x_ref[lo:hi, :], bm_ref[lo:hi, :], cm_ref[lo:hi, :]], axis=1)

    hist = jnp.concatenate([xh_ref[...], bh_ref[...], ch_ref[...]], axis=1)
    hist = jnp.where(first, jnp.zeros_like(hist), hist)

    tri2 = tri2_ref[...]
    shift = shift_ref[...]
    exm = ex_ref[...]
    cw = cw_ref[...]
    cbias = cb_ref[...]
    dtb = dtb_ref[...]
    a_row = arow_ref[...]
    d_exp = dexp_ref[...]
    ii = lax.broadcasted_iota(jnp.int32, (ln, ln), 0)
    jj = lax.broadcasted_iota(jnp.int32, (ln, ln), 1)
    causal = jj <= ii
    low_half = lax.broadcasted_iota(jnp.int32, (ln, 2 * SSM_P), 1) < SSM_P
    loc_shift = SMALL_COLS - DT_LANE0 - pl.program_id(1) * SSM_HPG

    def prep(ci):
        r0 = ci * ln
        if ci == 0:
            ext = jnp.concatenate([hist, raw_rows(0, ln)], axis=0)
        else:
            ext = raw_rows(r0 - hr, r0 + ln)
        taps = _dot(shift, ext)
        acc = cbias + ext[hr:, :].astype(F32) * cw[SSM_CONV - 1:SSM_CONV, :]
        for k in range(SSM_CONV - 1):
            acc = acc + taps[k * ln:(k + 1) * ln] * cw[k:k + 1, :]
        xbc = _silu_half(acc)
        xc = xbc[:, :gw]
        bc = xbc[:, gw:gw + SSM_N].astype(BF16)
        cc = xbc[:, gw + SSM_N:].astype(BF16)

        dt_small = _softplus(s_ref[r0:r0 + ln, :] + dtb)
        cs = _dot(tri2, _hi_lo(dt_small * a_row))
        cum_small = cs[:ln, :SMALL_COLS] + cs[:ln, SMALL_COLS:]
        rcum_small = cs[ln:, :SMALL_COLS] + cs[ln:, SMALL_COLS:]
        small3 = jnp.concatenate([dt_small.astype(BF16), jnp.exp2(cum_small).astype(BF16),
                                  jnp.exp2(rcum_small).astype(BF16)], axis=0)
        wide3 = _dot(small3, exm)
        cum_loc = pltpu.roll(cum_small, loc_shift, axis=1)
        cum_t = cum_loc.T
        return xc, bc, cc, wide3[:ln], wide3[ln:2 * ln], wide3[2 * ln:], cum_loc, cum_t

    def finish(ci, front):
        r0 = ci * ln
        xc, bc, cc, dt_exp, ecum, edst, cum_loc, cum_t = front
        xdt = xc * dt_exp
        xdt_b = xdt.astype(BF16)
        st = st_ref[...]
        y_off = _dot(cc, st.astype(BF16)) * ecum
        cbm = jnp.where(causal, _dot_nt(cc, bc), 0.0)
        yd = []
        for p in range(SSM_HPG // 2):
            ws, xm = [], []
            xp = xdt_b[:, 2 * p * SSM_P:(2 * p + 2) * SSM_P]
            for j, keep in ((2 * p, low_half), (2 * p + 1, ~low_half)):
                seg = cum_loc[:, j:j + 1] - cum_t[j:j + 1, :]
                ws.append((cbm * jnp.exp2(jnp.minimum(seg, 0.0))).astype(BF16))
                xm.append(jnp.where(keep, xp, jnp.zeros_like(xp)))
            yd.append(_dot(jnp.concatenate(ws, axis=1), jnp.concatenate(xm, axis=0)))
        y = y_off + jnp.concatenate(yd, axis=1)

        xd = (xdt * edst).astype(BF16)
        st_ref[...] = st * ecum[ln - 1:ln, :] + _dot_tn(bc, xd)

        o_ref[r0:r0 + ln, :] = (y + xc * d_exp).astype(BF16)

    front = prep(0)
    for ci in range(n_chunks):
        nxt = prep(ci + 1) if ci + 1 < n_chunks else None
        finish(ci, front)
        front = nxt


def _ssd(u, small, cw, cb, dtb_row, a_row, d_exp, exm, tri2, shift, batch, seq, tb):
    m = batch * seq
    nt = seq // tb
    row = lambda b, g, t: b * nt + t
    gcol = lambda off, width: (lambda b, g, t: (row(b, g, t), off // width + g))
    hist_row = lambda b, g, t: jnp.maximum(row(b, g, t) * (tb // HIST_ROWS) - 1, 0)
    hcol = lambda off, width: (lambda b, g, t: (hist_row(b, g, t), off // width + g))
    const2 = lambda shape: pl.BlockSpec(shape, lambda b, g, t: (0, 0))
    per_group = lambda r, c: pl.BlockSpec((None, r, c), lambda b, g, t: (g, 0, 0))
    kern = functools.partial(_ssd_kernel, n_chunks=tb // SSM_CHUNK, tb=tb)
    return pl.pallas_call(
        kern,
        grid=(batch, SSM_GROUPS, nt),
        in_specs=[
            pl.BlockSpec((tb, SSM_GW), gcol(OFF_X, SSM_GW)),
            pl.BlockSpec((tb, SSM_N), gcol(OFF_B, SSM_N)),
            pl.BlockSpec((tb, SSM_N), gcol(OFF_C, SSM_N)),
            pl.BlockSpec((HIST_ROWS, SSM_GW), hcol(OFF_X, SSM_GW)),
            pl.BlockSpec((HIST_ROWS, SSM_N), hcol(OFF_B, SSM_N)),
            pl.BlockSpec((HIST_ROWS, SSM_N), hcol(OFF_C, SSM_N)),
            pl.BlockSpec((tb, SMALL_COLS), lambda b, g, t: (row(b, g, t), 0)),
            per_group(SSM_CONV, SSM_CW),
            per_group(1, SSM_CW),
            const2((1, SMALL_COLS)),
            const2((1, SMALL_COLS)),
            pl.BlockSpec((1, SSM_GW), lambda b, g, t: (0, g)),
            per_group(SMALL_COLS, SSM_GW),
            const2((2 * SSM_CHUNK, SSM_CHUNK)),
            const2(((SSM_CONV - 1) * SSM_CHUNK, SSM_CHUNK + HIST_ROWS)),
        ],
        out_specs=pl.BlockSpec((tb, SSM_GW), lambda b, g, t: (row(b, g, t), g)),
        out_shape=jax.ShapeDtypeStruct((m, SSM_INNER), BF16),
        scratch_shapes=[pltpu.VMEM((SSM_N, SSM_GW), F32)],
        compiler_params=_params(("parallel", "parallel", "arbitrary")),
        name="ssd_mixer",
    )(u, u, u, u, u, u, small, cw, cb, dtb_row, a_row, d_exp, exm, tri2, shift)


def _group_rms(x, width):
    parts = []
    for c0 in range(0, x.shape[1], width):
        xg = x[:, c0:c0 + width]
        parts.append(xg * lax.rsqrt(jnp.mean(xg * xg, axis=-1, keepdims=True) + EPS))
    return jnp.concatenate(parts, axis=1)


def _merge_kernel(oa_ref, r_ref, ob_ref, z_ref, ga_ref, gb_ref, x_ref,
                  wa_ref, wb_ref, wo_ref, nw_ref, xo_ref, h_ref):
    oa = _group_rms(oa_ref[...].astype(F32), GLA_DV) * _silu_half(r_ref[...].astype(F32))
    ya = _dot(oa.astype(BF16), wa_ref[...])
    ob = ob_ref[...].astype(F32) * _silu_half(z_ref[...].astype(F32))
    yb = _dot(_group_rms(ob, SSM_GW).astype(BF16), wb_ref[...])
    mix = (_sigmoid2_half(ga_ref[...].astype(F32)) * ya
           + _sigmoid2_half(gb_ref[...].astype(F32)) * yb)
    xn = x_ref[...] + _dot(mix.astype(BF16), wo_ref[...])
    xo_ref[...] = xn
    h_ref[...] = _rms_rows(xn, nw_ref[...]).astype(BF16)


def _merge(oa, ob, u, x, wa, wb, wo, nw, tm):
    m = x.shape[0]
    full = lambda shape: pl.BlockSpec(shape, lambda i: (0, 0))
    return pl.pallas_call(
        _merge_kernel,
        grid=(m // tm,),
        in_specs=[pl.BlockSpec((tm, GLA_VAL), lambda i: (i, 0)),
                  pl.BlockSpec((tm, GLA_VAL), lambda i: (i, OFF_R // GLA_VAL)),
                  pl.BlockSpec((tm, SSM_INNER), lambda i: (i, 0)),
                  pl.BlockSpec((tm, SSM_INNER), lambda i: (i, OFF_Z // SSM_INNER)),
                  pl.BlockSpec((tm, D_MODEL), lambda i: (i, OFF_GA // D_MODEL)),
                  pl.BlockSpec((tm, D_MODEL), lambda i: (i, OFF_GB // D_MODEL)),
                  pl.BlockSpec((tm, D_MODEL), lambda i: (i, 0)),
                  full((GLA_VAL, D_MODEL)), full((SSM_INNER, D_MODEL)),
                  full((D_MODEL, D_MODEL)), full((1, D_MODEL))],
        out_specs=[pl.BlockSpec((tm, D_MODEL), lambda i: (i, 0)),
                   pl.BlockSpec((tm, D_MODEL), lambda i: (i, 0))],
        out_shape=[jax.ShapeDtypeStruct((m, D_MODEL), F32),
                   jax.ShapeDtypeStruct((m, D_MODEL), BF16)],
        compiler_params=_params(("parallel",)),
        name="merge_out",
    )(oa, u, ob, u, u, u, x, wa, wb, wo, nw)


def _ffn_kernel(h_ref, x_ref, wg_ref, wu_ref, wo_ref, nw_ref, o_ref, *, last):
    h = h_ref[...]
    gate = _dot(h, wg_ref[...])
    up = _dot(h, wu_ref[...])
    act = (_silu_half(gate) * up).astype(BF16)
    xn = x_ref[...] + _dot(act, wo_ref[...])
    if last:
        o_ref[...] = _rms_rows(xn, nw_ref[...])
    else:
        o_ref[0][...] = xn
        o_ref[1][...] = _rms_rows(xn, nw_ref[...]).astype(BF16)


def _ffn_kernel_mid(h_ref, x_ref, wg_ref, wu_ref, wo_ref, nw_ref, xo_ref, ho_ref):
    _ffn_kernel(h_ref, x_ref, wg_ref, wu_ref, wo_ref, nw_ref, (xo_ref, ho_ref), last=False)


def _ffn_kernel_last(h_ref, x_ref, wg_ref, wu_ref, wo_ref, nw_ref, o_ref):
    _ffn_kernel(h_ref, x_ref, wg_ref, wu_ref, wo_ref, nw_ref, o_ref, last=True)


def _ffn(h, x, wgu, wo, nw, tm, last):
    m = x.shape[0]
    full = lambda shape: pl.BlockSpec(shape, lambda i: (0, 0))
    rows = pl.BlockSpec((tm, D_MODEL), lambda i: (i, 0))
    if last:
        kern, out_specs = _ffn_kernel_last, rows
        out_shape = jax.ShapeDtypeStruct((m, D_MODEL), F32)
    else:
        kern, out_specs = _ffn_kernel_mid, [rows, rows]
        out_shape = [jax.ShapeDtypeStruct((m, D_MODEL), F32),
                     jax.ShapeDtypeStruct((m, D_MODEL), BF16)]
    return pl.pallas_call(
        kern,
        grid=(m // tm,),
        in_specs=[rows, rows, full((D_MODEL, FFN_HIDDEN)),
                  pl.BlockSpec((D_MODEL, FFN_HIDDEN), lambda i: (0, 1)),
                  full((FFN_HIDDEN, D_MODEL)), full((1, D_MODEL))],
        out_specs=out_specs,
        out_shape=out_shape,
        compiler_params=_params(("parallel",)),
        name="ffn_last" if last else "ffn",
    )(h, x, wgu, wgu, wo, nw)


def _tiles(batch, seq):
    m = batch * seq
    pick = lambda n, cands: next(c for c in cands if n % c == 0)
    return dict(
        norm=pick(m, (1024, 512, 256, 128)),
        inproj_m=pick(m, (1024, 512, 256, 128)),
        inproj_n=1024,
        gla=pick(seq, (1024, 512, 256, 128, 64)),
        ssd=pick(seq, (1024, 512, 256, 128)),
        merge=pick(m, (512, 256, 128)),
        ffn=pick(m, (256, 128)),
    )


def _ssd_group_cols(a):
    parts = []
    for g in range(SSM_GROUPS):
        b0 = SSM_INNER + g * SSM_N
        c0 = SSM_INNER + SSM_GROUPS * SSM_N + g * SSM_N
        parts.append(jnp.concatenate(
            [a[..., g * SSM_GW:(g + 1) * SSM_GW], a[..., b0:b0 + SSM_N], a[..., c0:c0 + SSM_N]],
            axis=-1))
    return jnp.stack(parts, axis=1)


def kernel(x, norm1_w, w_in, gla_gate_w2, gla_gate_b, gla_norm_w, ssm_conv_w, ssm_conv_b,
           ssm_dt_bias, ssm_A_log, ssm_D, ssm_norm_w, w_branch_a, w_branch_b, w_mix_out,
           norm2_w, w_ffn_in, w_ffn_out, final_norm_w):
    batch, seq, _ = x.shape
    depth = w_in.shape[0]
    m = batch * seq
    tl = _tiles(batch, seq)
    gla_mat = _gla_constants()
    ssd_ex, ssd_tri2, ssd_shift = _ssd_constants()

    offs = np.cumsum((0,) + IN_SPLITS)
    o_r, o_glr, o_z, o_xbc, o_dt, o_ga = (int(offs[i]) for i in (3, 4, 5, 6, 7, 8))
    IN_DIM = int(offs[-1])

    def small_rows(v):
        return jnp.pad(v, ((0, 0), (DT_LANE0, SMALL_COLS - DT_LANE0 - SSM_HEADS)))[:, None, :]

    w2_hi, w2_lo = _split_hi_lo(
        jnp.pad(gla_gate_w2, ((0, 0), (0, SMALL_COLS - GLA_RANK), (0, 0))))
    w2_all = jnp.concatenate([w2_hi, w2_hi, w2_lo], axis=1)
    conv_w_all = _ssd_group_cols(0.5 * ssm_conv_w)
    conv_b_all = _ssd_group_cols(0.5 * ssm_conv_b[:, None, :])
    dt_bias_all = small_rows(ssm_dt_bias)
    a_all = small_rows(-LOG2E * jnp.exp(ssm_A_log))
    d_all = jnp.repeat(ssm_D, SSM_P, axis=1)[:, None, :]

    gate_half = np.ones((1, IN_DIM), np.float32)
    gate_half[:, o_r:o_glr] = 0.5
    gate_half[:, o_z:o_xbc] = 0.5
    gate_half[:, o_ga:] = 0.5

    w_in_b = (w_in * gate_half).astype(BF16)
    wa_b = (w_branch_a * jnp.tile(gla_norm_w, (1, GLA_HEADS))[:, :, None]).astype(BF16)
    wb_b = (w_branch_b * ssm_norm_w[:, :, None]).astype(BF16)
    wmix_b = (0.5 * w_mix_out).astype(BF16)
    ffn_half = np.ones((1, 2 * FFN_HIDDEN), np.float32)
    ffn_half[:, :FFN_HIDDEN] = 0.5
    wffn_in_b = (w_ffn_in * ffn_half).astype(BF16)
    wffn_out_b = w_ffn_out.astype(BF16)

    xf = x.reshape(m, D_MODEL)
    h = _norm(xf, norm1_w[0].reshape(1, D_MODEL), tl["norm"])
    for l in range(depth):
        w = w_in_b[l]
        w_main = jnp.concatenate(
            [w[:, :o_r], w[:, o_z:o_xbc], w[:, o_r:o_glr], w[:, o_xbc:o_dt], w[:, o_ga:]], axis=1)
        w_small = jnp.concatenate(
            [w[:, o_glr:o_z], w[:, o_dt:o_ga],
             jnp.zeros((D_MODEL, SMALL_COLS - GLA_RANK - SSM_HEADS), BF16)], axis=1)
        u, small = _inproj(h, w_main, w_small, tl["inproj_m"], tl["inproj_n"])

        oa = _gla(u, small, w2_all[l], gla_gate_b[l].reshape(1, GLA_KEY), gla_mat,
                  batch, seq, tl["gla"])
        ob = _ssd(u, small, conv_w_all[l], conv_b_all[l], dt_bias_all[l], a_all[l], d_all[l],
                  ssd_ex, ssd_tri2, ssd_shift, batch, seq, tl["ssd"])
        xf, h2 = _merge(oa, ob, u, xf, wa_b[l], wb_b[l], wmix_b[l],
                        norm2_w[l].reshape(1, D_MODEL), tl["merge"])

        wgu, wo = wffn_in_b[l], wffn_out_b[l]
        if l + 1 < depth:
            xf, h = _ffn(h2, xf, wgu, wo, norm1_w[l + 1].reshape(1, D_MODEL), tl["ffn"], False)
        else:
            xf = _ffn(h2, xf, wgu, wo, final_norm_w.reshape(1, D_MODEL), tl["ffn"], True)
    return xf.reshape(batch, seq, D_MODEL)
```

```python
import functools

import numpy as np
import jax
import jax.numpy as jnp
from jax import lax
from jax.experimental import pallas as pl
from jax.experimental.pallas import tpu as pltpu

F32 = jnp.float32
BF16 = jnp.bfloat16

D_MODEL = 1024
EPS = 1e-6
GLA_HEADS = 4
GLA_DK = 128
GLA_DV = 256
GLA_KEY = GLA_HEADS * GLA_DK
GLA_VAL = GLA_HEADS * GLA_DV
GLA_RANK = 16
GLA_TAU = 16.0
GLA_CHUNK = 64
GLA_LEVELS = 6
GLA_MXU_LEVELS = 2
SSM_INNER = 2048
SSM_P = 64
SSM_HEADS = 32
SSM_GROUPS = 4
SSM_N = 128
SSM_CONV = 4
SSM_CHUNK = 128
SSM_HPG = SSM_HEADS // SSM_GROUPS
SSM_GW = SSM_HPG * SSM_P
SSM_CW = SSM_GW + 2 * SSM_N
SSM_XBC = SSM_INNER + 2 * SSM_GROUPS * SSM_N
FFN_HIDDEN = 2816
IN_SPLITS = (GLA_KEY, GLA_KEY, GLA_VAL, GLA_VAL, GLA_RANK, SSM_INNER, SSM_XBC,
             SSM_HEADS, D_MODEL, D_MODEL)

OFF_Q = 0
OFF_K = OFF_Q + GLA_KEY
OFF_V = OFF_K + GLA_KEY
OFF_Z = OFF_V + GLA_VAL
OFF_R = OFF_Z + SSM_INNER
OFF_X = OFF_R + GLA_VAL
OFF_B = OFF_X + SSM_INNER
OFF_C = OFF_B + SSM_GROUPS * SSM_N
OFF_GA = OFF_C + SSM_GROUPS * SSM_N
OFF_GB = OFF_GA + D_MODEL
U_COLS = OFF_GB + D_MODEL
SMALL_COLS = 128
DT_LANE0 = GLA_RANK
HIST_ROWS = 16

LOG2E = 1.4426950408889634

VMEM_LIMIT = 48 * 1024 * 1024


def _silu_half(hx):
    return hx * jnp.tanh(hx) + hx


def _sigmoid2_half(hx):
    return jnp.tanh(hx) + 1.0


def _softplus(x):
    return jnp.maximum(x, 0.0) + jnp.log1p(jnp.exp(-jnp.abs(x)))


def _log_sigmoid(x):
    return jnp.minimum(x, 0.0) - jnp.log1p(jnp.exp(-jnp.abs(x)))


def _split_hi_lo(x):
    hi = x.astype(BF16)
    lo = (x - hi.astype(F32)).astype(BF16)
    return hi, lo


def _hi_lo(x):
    hi, lo = _split_hi_lo(x)
    return jnp.concatenate([hi, lo], axis=1)


def _dot(a, b):
    return jnp.dot(a, b, preferred_element_type=F32)


def _dot_nt(a, b):
    return lax.dot_general(a, b, (((1,), (1,)), ((), ())), preferred_element_type=F32)


def _dot_tn(a, b):
    return lax.dot_general(a, b, (((0,), (0,)), ((), ())), preferred_element_type=F32)


def _params(semantics):
    return pltpu.CompilerParams(dimension_semantics=semantics, vmem_limit_bytes=VMEM_LIMIT)


def _rms_rows(x, w):
    return x * lax.rsqrt(jnp.mean(x * x, axis=-1, keepdims=True) + EPS) * w


def _norm_kernel(x_ref, w_ref, h_ref):
    h_ref[...] = _rms_rows(x_ref[...], w_ref[...]).astype(BF16)


def _norm(x, w, tm):
    m = x.shape[0]
    return pl.pallas_call(
        _norm_kernel,
        grid=(m // tm,),
        in_specs=[pl.BlockSpec((tm, D_MODEL), lambda i: (i, 0)),
                  pl.BlockSpec((1, D_MODEL), lambda i: (0, 0))],
        out_specs=pl.BlockSpec((tm, D_MODEL), lambda i: (i, 0)),
        out_shape=jax.ShapeDtypeStruct((m, D_MODEL), BF16),
        compiler_params=_params(("parallel",)),
        name="rms_norm",
    )(x, w)


def _inproj_kernel(h_ref, w_ref, ws_ref, u_ref, s_ref):
    h = h_ref[...]
    u_ref[...] = _dot(h, w_ref[...]).astype(BF16)

    @pl.when(pl.program_id(1) == 0)
    def _():
        s_ref[...] = _dot(h, ws_ref[...])


def _inproj(h, w_main, w_small, tm, tn):
    m = h.shape[0]
    return pl.pallas_call(
        _inproj_kernel,
        grid=(m // tm, U_COLS // tn),
        in_specs=[pl.BlockSpec((tm, D_MODEL), lambda i, j: (i, 0)),
                  pl.BlockSpec((D_MODEL, tn), lambda i, j: (0, j)),
                  pl.BlockSpec((D_MODEL, SMALL_COLS), lambda i, j: (0, 0))],
        out_specs=[pl.BlockSpec((tm, tn), lambda i, j: (i, j)),
                   pl.BlockSpec((tm, SMALL_COLS), lambda i, j: (i, 0))],
        out_shape=[jax.ShapeDtypeStruct((m, U_COLS), BF16),
                   jax.ShapeDtypeStruct((m, SMALL_COLS), F32)],
        compiler_params=_params(("parallel", "arbitrary")),
        name="in_proj",
    )(h, w_main, w_small)


def _gla_constants():
    c = GLA_CHUNK
    mat = np.zeros((1 + GLA_MXU_LEVELS, c, c), np.float32)
    t = np.arange(c)[:, None]
    tau = np.arange(c)[None, :]
    mat[0] = tau <= t
    for lvl in range(GLA_MXU_LEVELS):
        s = 1 << lvl
        mid = (t // (2 * s)) * (2 * s) + s
        upper = t >= mid
        mat[1 + lvl] = np.where(upper, (tau >= mid) & (tau <= t), (tau > t) & (tau <= mid - 1))
    return jnp.asarray(mat.reshape((1 + GLA_MXU_LEVELS) * c, c), BF16)


def _gla_level_mask(lvl):
    c = GLA_CHUNK
    s = 1 << lvl
    i = lax.broadcasted_iota(jnp.int32, (c, c), 0)
    j = lax.broadcasted_iota(jnp.int32, (c, c), 1)
    same = (i // (2 * s)) == (j // (2 * s))
    return same & ((i % (2 * s)) >= s) & ((j % (2 * s)) < s)


def _gla_kernel(q_ref, k_ref, v_ref, s_ref, w2_ref, b2_ref, mat_ref, o_ref, st_ref, *, n_chunks):
    c = GLA_CHUNK

    @pl.when(pl.program_id(1) == 0)
    def _():
        st_ref[...] = jnp.zeros_like(st_ref)

    w2 = w2_ref[...]
    b2 = b2_ref[...]
    mat = mat_ref[...]
    ii = lax.broadcasted_iota(jnp.int32, (c, c), 0)
    jj = lax.broadcasted_iota(jnp.int32, (c, c), 1)
    masks = [_gla_level_mask(lvl) for lvl in range(GLA_LEVELS)]
    eye = ii == jj
    heads = range(GLA_HEADS)
    kcs = [slice(h * GLA_DK, (h + 1) * GLA_DK) for h in heads]
    vcs = [slice(h * GLA_DV, (h + 1) * GLA_DV) for h in heads]

    def ref_rows(b, s):
        blk = b.reshape(c // (2 * s), 2 * s, GLA_DK)
        return jnp.broadcast_to(blk[:, s - 1:s, :], blk.shape).reshape(c, GLA_DK)

    def chunk(ci, carry):
        r0 = pl.multiple_of(ci * c, c)
        rows = pl.ds(r0, c)
        l_hi, l_lo = _split_hi_lo(s_ref[rows, :])
        logit = _dot(jnp.concatenate([l_hi, l_lo, l_hi], axis=1), w2) + b2
        g_all = _log_sigmoid(logit) * (LOG2E / GLA_TAU)

        x_inter, x_state, x_lvl, d_col = [], [], [], []
        for h in heads:
            e2 = _dot(mat, _hi_lo(g_all[:, kcs[h]]))
            e = e2[:, :GLA_DK] + e2[:, GLA_DK:]
            b = e[0:c]
            x_inter.append(jnp.exp2(b))
            x_state.append(jnp.exp2(b[c - 1:c] - b))
            lv = [jnp.exp2(e[(1 + l) * c:(2 + l) * c]) for l in range(GLA_MXU_LEVELS)]
            lv += [jnp.exp2(-jnp.abs(b - ref_rows(b, 1 << l)))
                   for l in range(GLA_MXU_LEVELS, GLA_LEVELS)]
            x_lvl.append(lv)
            d_col.append(jnp.exp2(b.T[:, c - 1:c]))

        amat = []
        for h in heads:
            q = q_ref[rows, kcs[h]].astype(F32) * (GLA_DK ** -0.5)
            k = k_ref[rows, kcs[h]].astype(F32)
            a = jnp.where(eye, _dot_nt(q.astype(BF16), k.astype(BF16)), 0.0)
            for lvl in range(GLA_LEVELS):
                xl = x_lvl[h][lvl]
                p = _dot_nt((q * xl).astype(BF16), (k * xl).astype(BF16))
                a = a + jnp.where(masks[lvl], p, 0.0)
            amat.append(a.astype(BF16))

        for h in heads:
            q = q_ref[rows, kcs[h]].astype(F32) * (GLA_DK ** -0.5)
            k = k_ref[rows, kcs[h]].astype(F32)
            v = v_ref[rows, vcs[h]]
            st = st_ref[h]
            o = _dot((q * x_inter[h]).astype(BF16), st.astype(BF16)) + _dot(amat[h], v)
            o_ref[rows, vcs[h]] = o.astype(BF16)
            ks = (k * x_state[h]).astype(BF16)
            st_ref[h] = st * d_col[h] + _dot_tn(ks, v)
        return carry

    lax.fori_loop(0, n_chunks, chunk, 0, unroll=4)


def _gla(u, small, w2, b2, mat, batch, seq, tb):
    m = batch * seq
    nt = seq // tb
    row = lambda b, t: b * nt + t
    kern = functools.partial(_gla_kernel, n_chunks=tb // GLA_CHUNK)
    return pl.pallas_call(
        kern,
        grid=(batch, nt),
        in_specs=[
            pl.BlockSpec((tb, GLA_KEY), lambda b, t: (row(b, t), OFF_Q // GLA_KEY)),
            pl.BlockSpec((tb, GLA_KEY), lambda b, t: (row(b, t), OFF_K // GLA_KEY)),
            pl.BlockSpec((tb, GLA_VAL), lambda b, t: (row(b, t), OFF_V // GLA_VAL)),
            pl.BlockSpec((tb, SMALL_COLS), lambda b, t: (row(b, t), 0)),
            pl.BlockSpec((3 * SMALL_COLS, GLA_KEY), lambda b, t: (0, 0)),
            pl.BlockSpec((1, GLA_KEY), lambda b, t: (0, 0)),
            pl.BlockSpec(((1 + GLA_MXU_LEVELS) * GLA_CHUNK, GLA_CHUNK), lambda b, t: (0, 0)),
        ],
        out_specs=pl.BlockSpec((tb, GLA_VAL), lambda b, t: (row(b, t), 0)),
        out_shape=jax.ShapeDtypeStruct((m, GLA_VAL), BF16),
        scratch_shapes=[pltpu.VMEM((GLA_HEADS, GLA_DK, GLA_DV), F32)],
        compiler_params=_params(("parallel", "arbitrary")),
        name="gla_mixer",
    )(u, u, u, small, w2, b2, mat)


def _ssd_constants():
    ex = np.zeros((SSM_GROUPS, SMALL_COLS, SSM_GW), np.float32)
    for g in range(SSM_GROUPS):
        for j in range(SSM_HPG):
            ex[g, DT_LANE0 + g * SSM_HPG + j, j * SSM_P:(j + 1) * SSM_P] = 1.0
    ln = SSM_CHUNK
    tril = np.tril(np.ones((ln, ln), np.float32))
    tri2 = np.concatenate([tril, 1.0 - tril], axis=0)
    shift = np.zeros((SSM_CONV - 1, ln, ln + HIST_ROWS), np.float32)
    for k in range(SSM_CONV - 1):
        shift[k, np.arange(ln), HIST_ROWS + np.arange(ln) - (SSM_CONV - 1) + k] = 1.0
    return (jnp.asarray(ex, BF16), jnp.asarray(tri2, BF16),
            jnp.asarray(shift.reshape((SSM_CONV - 1) * ln, ln + HIST_ROWS), BF16))


def _ssd_kernel(x_ref, bm_ref, cm_ref, xh_ref, bh_ref, ch_ref, s_ref, cw_ref, cb_ref,
                dtb_ref, arow_ref, dexp_ref, ex_ref, tri2_ref, shift_ref,
                o_ref, st_ref, *, n_chunks, tb):
    ln = SSM_CHUNK
    hr = HIST_ROWS
    gw = SSM_GW
    first = pl.program_id(2) == 0

    @pl.when(first)
    def _():
        st_ref[...] = jnp.zeros_like(st_ref)

    def raw_rows(lo, hi):
        return jnp.concatenate([x_ref[lo:hi, :], bm_ref[lo:hi, :], cm_ref[lo:hi, :]], axis=1)

    hist = jnp.concatenate([xh_ref[...], bh_ref[...], ch_ref[...]], axis=1)
    hist = jnp.where(first, jnp.zeros_like(hist), hist)

    tri2 = tri2_ref[...]
    shift = shift_ref[...]
    exm = ex_ref[...]
    cw = cw_ref[...]
    cbias = cb_ref[...]
    dtb = dtb_ref[...]
    a_row = arow_ref[...]
    d_exp = dexp_ref[...]
    ii = lax.broadcasted_iota(jnp.int32, (ln, ln), 0)
    jj = lax.broadcasted_iota(jnp.int32, (ln, ln), 1)
    causal = jj <= ii
    low_half = lax.broadcasted_iota(jnp.int32, (ln, 2 * SSM_P), 1) < SSM_P
    loc_shift = SMALL_COLS - DT_LANE0 - pl.program_id(1) * SSM_HPG

    def prep(ci):
        r0 = ci * ln
        if ci == 0:
            ext = jnp.concatenate([hist, raw_rows(0, ln)], axis=0)
        else:
            ext = raw_rows(r0 - hr, r0 + ln)
        taps = _dot(shift, ext)
        acc = cbias + ext[hr:, :].astype(F32) * cw[SSM_CONV - 1:SSM_CONV, :]
        for k in range(SSM_CONV - 1):
            acc = acc + taps[k * ln:(k + 1) * ln] * cw[k:k + 1, :]
        xbc = _silu_half(acc)
        xc = xbc[:, :gw]
        bc = xbc[:, gw:gw + SSM_N].astype(BF16)
        cc = xbc[:, gw + SSM_N:].astype(BF16)

        dt_small = _softplus(s_ref[r0:r0 + ln, :] + dtb)
        cs = _dot(tri2, _hi_lo(dt_small * a_row))
        cum_small = cs[:ln, :SMALL_COLS] + cs[:ln, SMALL_COLS:]
        rcum_small = cs[ln:, :SMALL_COLS] + cs[ln:, SMALL_COLS:]
        small3 = jnp.concatenate([dt_small.astype(BF16), jnp.exp2(cum_small).astype(BF16),
                                  jnp.exp2(rcum_small).astype(BF16)], axis=0)
        wide3 = _dot(small3, exm)
        cum_loc = pltpu.roll(cum_small, loc_shift, axis=1)
        cum_t = cum_loc.T
        return xc, bc, cc, wide3[:ln], wide3[ln:2 * ln], wide3[2 * ln:], cum_loc, cum_t

    def finish(ci, front):
        r0 = ci * ln
        xc, bc, cc, dt_exp, ecum, edst, cum_loc, cum_t = front
        xdt = xc * dt_exp
        xdt_b = xdt.astype(BF16)
        st = st_ref[...]
        y_off = _dot(cc, st.astype(BF16)) * ecum
        cbm = jnp.where(causal, _dot_nt(cc, bc), 0.0)
        yd = []
        for p in range(SSM_HPG // 2):
            ws, xm = [], []
            xp = xdt_b[:, 2 * p * SSM_P:(2 * p + 2) * SSM_P]
            for j, keep in ((2 * p, low_half), (2 * p + 1, ~low_half)):
                seg = cum_loc[:, j:j + 1] - cum_t[j:j + 1, :]
                ws.append((cbm * jnp.exp2(jnp.minimum(seg, 0.0))).astype(BF16))
                xm.append(jnp.where(keep, xp, jnp.zeros_like(xp)))
            yd.append(_dot(jnp.concatenate(ws, axis=1), jnp.concatenate(xm, axis=0)))
        y = y_off + jnp.concatenate(yd, axis=1)

        xd = (xdt * edst).astype(BF16)
        st_ref[...] = st * ecum[ln - 1:ln, :] + _dot_tn(bc, xd)

        o_ref[r0:r0 + ln, :] = (y + xc * d_exp).astype(BF16)

    front = prep(0)
    for ci in range(n_chunks):
        nxt = prep(ci + 1) if ci + 1 < n_chunks else None
        finish(ci, front)
        front = nxt


def _ssd(u, small, cw, cb, dtb_row, a_row, d_exp, exm, tri2, shift, batch, seq, tb):
    m = batch * seq
    nt = seq // tb
    row = lambda b, g, t: b * nt + t
    gcol = lambda off, width: (lambda b, g, t: (row(b, g, t), off // width + g))
    hist_row = lambda b, g, t: jnp.maximum(row(b, g, t) * (tb // HIST_ROWS) - 1, 0)
    hcol = lambda off, width: (lambda b, g, t: (hist_row(b, g, t), off // width + g))
    const2 = lambda shape: pl.BlockSpec(shape, lambda b, g, t: (0, 0))
    per_group = lambda r, c: pl.BlockSpec((None, r, c), lambda b, g, t: (g, 0, 0))
    kern = functools.partial(_ssd_kernel, n_chunks=tb // SSM_CHUNK, tb=tb)
    return pl.pallas_call(
        kern,
        grid=(batch, SSM_GROUPS, nt),
        in_specs=[
            pl.BlockSpec((tb, SSM_GW), gcol(OFF_X, SSM_GW)),
            pl.BlockSpec((tb, SSM_N), gcol(OFF_B, SSM_N)),
            pl.BlockSpec((tb, SSM_N), gcol(OFF_C, SSM_N)),
            pl.BlockSpec((HIST_ROWS, SSM_GW), hcol(OFF_X, SSM_GW)),
            pl.BlockSpec((HIST_ROWS, SSM_N), hcol(OFF_B, SSM_N)),
            pl.BlockSpec((HIST_ROWS, SSM_N), hcol(OFF_C, SSM_N)),
            pl.BlockSpec((tb, SMALL_COLS), lambda b, g, t: (row(b, g, t), 0)),
            per_group(SSM_CONV, SSM_CW),
            per_group(1, SSM_CW),
            const2((1, SMALL_COLS)),
            const2((1, SMALL_COLS)),
            pl.BlockSpec((1, SSM_GW), lambda b, g, t: (0, g)),
            per_group(SMALL_COLS, SSM_GW),
            const2((2 * SSM_CHUNK, SSM_CHUNK)),
            const2(((SSM_CONV - 1) * SSM_CHUNK, SSM_CHUNK + HIST_ROWS)),
        ],
        out_specs=pl.BlockSpec((tb, SSM_GW), lambda b, g, t: (row(b, g, t), g)),
        out_shape=jax.ShapeDtypeStruct((m, SSM_INNER), BF16),
        scratch_shapes=[pltpu.VMEM((SSM_N, SSM_GW), F32)],
        compiler_params=_params(("parallel", "parallel", "arbitrary")),
        name="ssd_mixer",
    )(u, u, u, u, u, u, small, cw, cb, dtb_row, a_row, d_exp, exm, tri2, shift)


def _group_rms(x, width):
    parts = []
    for c0 in range(0, x.shape[1], width):
        xg = x[:, c0:c0 + width]
        parts.append(xg * lax.rsqrt(jnp.mean(xg * xg, axis=-1, keepdims=True) + EPS))
    return jnp.concatenate(parts, axis=1)


def _merge_kernel(oa_ref, r_ref, ob_ref, z_ref, ga_ref, gb_ref, x_ref,
                  wa_ref, wb_ref, wo_ref, nw_ref, xo_ref, h_ref):
    oa = _group_rms(oa_ref[...].astype(F32), GLA_DV) * _silu_half(r_ref[...].astype(F32))
    ya = _dot(oa.astype(BF16), wa_ref[...])
    ob = ob_ref[...].astype(F32) * _silu_half(z_ref[...].astype(F32))
    yb = _dot(_group_rms(ob, SSM_GW).astype(BF16), wb_ref[...])
    mix = (_sigmoid2_half(ga_ref[...].astype(F32)) * ya
           + _sigmoid2_half(gb_ref[...].astype(F32)) * yb)
    xn = x_ref[...] + _dot(mix.astype(BF16), wo_ref[...])
    xo_ref[...] = xn
    h_ref[...] = _rms_rows(xn, nw_ref[...]).astype(BF16)


def _merge(oa, ob, u, x, wa, wb, wo, nw, tm):
    m = x.shape[0]
    full = lambda shape: pl.BlockSpec(shape, lambda i: (0, 0))
    return pl.pallas_call(
        _merge_kernel,
        grid=(m // tm,),
        in_specs=[pl.BlockSpec((tm, GLA_VAL), lambda i: (i, 0)),
                  pl.BlockSpec((tm, GLA_VAL), lambda i: (i, OFF_R // GLA_VAL)),
                  pl.BlockSpec((tm, SSM_INNER), lambda i: (i, 0)),
                  pl.BlockSpec((tm, SSM_INNER), lambda i: (i, OFF_Z // SSM_INNER)),
                  pl.BlockSpec((tm, D_MODEL), lambda i: (i, OFF_GA // D_MODEL)),
                  pl.BlockSpec((tm, D_MODEL), lambda i: (i, OFF_GB // D_MODEL)),
                  pl.BlockSpec((tm, D_MODEL), lambda i: (i, 0)),
                  full((GLA_VAL, D_MODEL)), full((SSM_INNER, D_MODEL)),
                  full((D_MODEL, D_MODEL)), full((1, D_MODEL))],
        out_specs=[pl.BlockSpec((tm, D_MODEL), lambda i: (i, 0)),
                   pl.BlockSpec((tm, D_MODEL), lambda i: (i, 0))],
        out_shape=[jax.ShapeDtypeStruct((m, D_MODEL), F32),
                   jax.ShapeDtypeStruct((m, D_MODEL), BF16)],
        compiler_params=_params(("parallel",)),
        name="merge_out",
    )(oa, u, ob, u, u, u, x, wa, wb, wo, nw)


def _ffn_kernel(h_ref, x_ref, wg_ref, wu_ref, wo_ref, nw_ref, o_ref, *, last):
    h = h_ref[...]
    gate = _dot(h, wg_ref[...])
    up = _dot(h, wu_ref[...])
    act = (_silu_half(gate) * up).astype(BF16)
    xn = x_ref[...] + _dot(act, wo_ref[...])
    if last:
        o_ref[...] = _rms_rows(xn, nw_ref[...])
    else:
        o_ref[0][...] = xn
        o_ref[1][...] = _rms_rows(xn, nw_ref[...]).astype(BF16)


def _ffn_kernel_mid(h_ref, x_ref, wg_ref, wu_ref, wo_ref, nw_ref, xo_ref, ho_ref):
    _ffn_kernel(h_ref, x_ref, wg_ref, wu_ref, wo_ref, nw_ref, (xo_ref, ho_ref), last=False)


def _ffn_kernel_last(h_ref, x_ref, wg_ref, wu_ref, wo_ref, nw_ref, o_ref):
    _ffn_kernel(h_ref, x_ref, wg_ref, wu_ref, wo_ref, nw_ref, o_ref, last=True)


def _ffn(h, x, wgu, wo, nw, tm, last):
    m = x.shape[0]
    once = pl.Buffered(1)
    full = lambda shape, col=0: pl.BlockSpec(shape, lambda i: (0, col), pipeline_mode=once)
    rows = pl.BlockSpec((tm, D_MODEL), lambda i: (i, 0))
    if last:
        kern, out_specs = _ffn_kernel_last, rows
        out_shape = jax.ShapeDtypeStruct((m, D_MODEL), F32)
    else:
        kern, out_specs = _ffn_kernel_mid, [rows, rows]
        out_shape = [jax.ShapeDtypeStruct((m, D_MODEL), F32),
                     jax.ShapeDtypeStruct((m, D_MODEL), BF16)]
    return pl.pallas_call(
        kern,
        grid=(m // tm,),
        in_specs=[rows, rows, full((D_MODEL, FFN_HIDDEN)), full((D_MODEL, FFN_HIDDEN), 1),
                  full((FFN_HIDDEN, D_MODEL)), full((1, D_MODEL))],
        out_specs=out_specs,
        out_shape=out_shape,
        compiler_params=_params(("parallel",)),
        name="ffn_last" if last else "ffn",
    )(h, x, wgu, wgu, wo, nw)


def _tiles(batch, seq):
    m = batch * seq
    pick = lambda n, cands: next(c for c in cands if n % c == 0)
    return dict(
        norm=pick(m, (1024, 512, 256, 128)),
        inproj_m=pick(m, (1024, 512, 256, 128)),
        inproj_n=2048,
        gla=pick(seq, (1024, 512, 256, 128, 64)),
        ssd=pick(seq, (1024, 512, 256, 128)),
        merge=pick(m, (512, 256, 128)),
        ffn=pick(m, (512, 256, 128)),
    )


def _ssd_group_cols(a):
    parts = []
    for g in range(SSM_GROUPS):
        b0 = SSM_INNER + g * SSM_N
        c0 = SSM_INNER + SSM_GROUPS * SSM_N + g * SSM_N
        parts.append(jnp.concatenate(
            [a[..., g * SSM_GW:(g + 1) * SSM_GW], a[..., b0:b0 + SSM_N], a[..., c0:c0 + SSM_N]],
            axis=-1))
    return jnp.stack(parts, axis=1)


def kernel(x, norm1_w, w_in, gla_gate_w2, gla_gate_b, gla_norm_w, ssm_conv_w, ssm_conv_b,
           ssm_dt_bias, ssm_A_log, ssm_D, ssm_norm_w, w_branch_a, w_branch_b, w_mix_out,
           norm2_w, w_ffn_in, w_ffn_out, final_norm_w):
    batch, seq, _ = x.shape
    depth = w_in.shape[0]
    m = batch * seq
    tl = _tiles(batch, seq)
    gla_mat = _gla_constants()
    ssd_ex, ssd_tri2, ssd_shift = _ssd_constants()

    offs = np.cumsum((0,) + IN_SPLITS)
    o_r, o_glr, o_z, o_xbc, o_dt, o_ga = (int(offs[i]) for i in (3, 4, 5, 6, 7, 8))
    IN_DIM = int(offs[-1])

    def small_rows(v):
        return jnp.pad(v, ((0, 0), (DT_LANE0, SMALL_COLS - DT_LANE0 - SSM_HEADS)))[:, None, :]

    w2_hi, w2_lo = _split_hi_lo(
        jnp.pad(gla_gate_w2, ((0, 0), (0, SMALL_COLS - GLA_RANK), (0, 0))))
    w2_all = jnp.concatenate([w2_hi, w2_hi, w2_lo], axis=1)
    conv_w_all = _ssd_group_cols(0.5 * ssm_conv_w)
    conv_b_all = _ssd_group_cols(0.5 * ssm_conv_b[:, None, :])
    dt_bias_all = small_rows(ssm_dt_bias)
    a_all = small_rows(-LOG2E * jnp.exp(ssm_A_log))
    d_all = jnp.repeat(ssm_D, SSM_P, axis=1)[:, None, :]

    gate_half = np.ones((1, IN_DIM), np.float32)
    gate_half[:, o_r:o_glr] = 0.5
    gate_half[:, o_z:o_xbc] = 0.5
    gate_half[:, o_ga:] = 0.5

    w_in_b = (w_in * gate_half).astype(BF16)
    wa_b = (w_branch_a * jnp.tile(gla_norm_w, (1, GLA_HEADS))[:, :, None]).astype(BF16)
    wb_b = (w_branch_b * ssm_norm_w[:, :, None]).astype(BF16)
    wmix_b = (0.5 * w_mix_out).astype(BF16)
    ffn_half = np.ones((1, 2 * FFN_HIDDEN), np.float32)
    ffn_half[:, :FFN_HIDDEN] = 0.5
    wffn_in_b = (w_ffn_in * ffn_half).astype(BF16)
    wffn_out_b = w_ffn_out.astype(BF16)

    xf = x.reshape(m, D_MODEL)
    h = _norm(xf, norm1_w[0].reshape(1, D_MODEL), tl["norm"])
    for l in range(depth):
        w = w_in_b[l]
        w_main = jnp.concatenate(
            [w[:, :o_r], w[:, o_z:o_xbc], w[:, o_r:o_glr], w[:, o_xbc:o_dt], w[:, o_ga:]], axis=1)
        w_small = jnp.concatenate(
            [w[:, o_glr:o_z], w[:, o_dt:o_ga],
             jnp.zeros((D_MODEL, SMALL_COLS - GLA_RANK - SSM_HEADS), BF16)], axis=1)
        u, small = _inproj(h, w_main, w_small, tl["inproj_m"], tl["inproj_n"])

        oa = _gla(u, small, w2_all[l], gla_gate_b[l].reshape(1, GLA_KEY), gla_mat,
                  batch, seq, tl["gla"])
        ob = _ssd(u, small, conv_w_all[l], conv_b_all[l], dt_bias_all[l], a_all[l], d_all[l],
                  ssd_ex, ssd_tri2, ssd_shift, batch, seq, tl["ssd"])
        xf, h2 = _merge(oa, ob, u, xf, wa_b[l], wb_b[l], wmix_b[l],
                        norm2_w[l].reshape(1, D_MODEL), tl["merge"])

        wgu, wo = wffn_in_b[l], wffn_out_b[l]
        if l + 1 < depth:
            xf, h = _ffn(h2, xf, wgu, wo, norm1_w[l + 1].reshape(1, D_MODEL), tl["ffn"], False)
        else:
            xf = _ffn(h2, xf, wgu, wo, final_norm_w.reshape(1, D_MODEL), tl["ffn"], True)
    return xf.reshape(batch, seq, D_MODEL)
```

```python
import functools

import numpy as np
import jax
import jax.numpy as jnp
from jax import lax
from jax.experimental import pallas as pl
from jax.experimental.pallas import tpu as pltpu

F32 = jnp.float32
BF16 = jnp.bfloat16

D_MODEL = 1024
EPS = 1e-6
GLA_HEADS = 4
GLA_DK = 128
GLA_DV = 256
GLA_KEY = GLA_HEADS * GLA_DK
GLA_VAL = GLA_HEADS * GLA_DV
GLA_RANK = 16
GLA_TAU = 16.0
GLA_CHUNK = 64
GLA_LEVELS = 6
GLA_MXU_LEVELS = 2
SSM_INNER = 2048
SSM_P = 64
SSM_HEADS = 32
SSM_GROUPS = 4
SSM_N = 128
SSM_CONV = 4
SSM_CHUNK = 128
SSM_HPG = SSM_HEADS // SSM_GROUPS
SSM_GW = SSM_HPG * SSM_P
SSM_CW = SSM_GW + 2 * SSM_N
SSM_XBC = SSM_INNER + 2 * SSM_GROUPS * SSM_N
FFN_HIDDEN = 2816
IN_SPLITS = (GLA_KEY, GLA_KEY, GLA_VAL, GLA_VAL, GLA_RANK, SSM_INNER, SSM_XBC,
             SSM_HEADS, D_MODEL, D_MODEL)

OFF_Q = 0
OFF_K = OFF_Q + GLA_KEY
OFF_V = OFF_K + GLA_KEY
OFF_Z = OFF_V + GLA_VAL
OFF_R = OFF_Z + SSM_INNER
OFF_X = OFF_R + GLA_VAL
OFF_B = OFF_X + SSM_INNER
OFF_C = OFF_B + SSM_GROUPS * SSM_N
OFF_GA = OFF_C + SSM_GROUPS * SSM_N
OFF_GB = OFF_GA + D_MODEL
U_COLS = OFF_GB + D_MODEL
SMALL_COLS = 128
DT_LANE0 = GLA_RANK
HIST_ROWS = 16

LOG2E = 1.4426950408889634

VMEM_LIMIT = 48 * 1024 * 1024


def _silu_half(hx):
    return hx * jnp.tanh(hx) + hx


def _sigmoid2_half(hx):
    return jnp.tanh(hx) + 1.0


def _softplus(x):
    return jnp.maximum(x, 0.0) + jnp.log1p(jnp.exp(-jnp.abs(x)))


def _log_sigmoid(x):
    return jnp.minimum(x, 0.0) - jnp.log1p(jnp.exp(-jnp.abs(x)))


def _split_hi_lo(x):
    hi = x.astype(BF16)
    lo = (x - hi.astype(F32)).astype(BF16)
    return hi, lo


def _hi_lo(x):
    hi, lo = _split_hi_lo(x)
    return jnp.concatenate([hi, lo], axis=1)


def _dot(a, b):
    return jnp.dot(a, b, preferred_element_type=F32)


def _dot_nt(a, b):
    return lax.dot_general(a, b, (((1,), (1,)), ((), ())), preferred_element_type=F32)


def _dot_tn(a, b):
    return lax.dot_general(a, b, (((0,), (0,)), ((), ())), preferred_element_type=F32)


def _params(semantics):
    return pltpu.CompilerParams(dimension_semantics=semantics, vmem_limit_bytes=VMEM_LIMIT)


def _rms_rows(x, w):
    return x * lax.rsqrt(jnp.mean(x * x, axis=-1, keepdims=True) + EPS) * w


def _norm_kernel(x_ref, w_ref, h_ref):
    h_ref[...] = _rms_rows(x_ref[...], w_ref[...]).astype(BF16)


def _norm(x, w, tm):
    m = x.shape[0]
    return pl.pallas_call(
        _norm_kernel,
        grid=(m // tm,),
        in_specs=[pl.BlockSpec((tm, D_MODEL), lambda i: (i, 0)),
                  pl.BlockSpec((1, D_MODEL), lambda i: (0, 0))],
        out_specs=pl.BlockSpec((tm, D_MODEL), lambda i: (i, 0)),
        out_shape=jax.ShapeDtypeStruct((m, D_MODEL), BF16),
        compiler_params=_params(("parallel",)),
        name="rms_norm",
    )(x, w)


def _inproj_kernel(h_ref, w_ref, ws_ref, u_ref, s_ref):
    h = h_ref[...]
    u_ref[...] = _dot(h, w_ref[...]).astype(BF16)

    @pl.when(pl.program_id(1) == 0)
    def _():
        s_ref[...] = _dot(h, ws_ref[...])


def _inproj(h, w_main, w_small, tm, tn):
    m = h.shape[0]
    return pl.pallas_call(
        _inproj_kernel,
        grid=(m // tm, U_COLS // tn),
        in_specs=[pl.BlockSpec((tm, D_MODEL), lambda i, j: (i, 0)),
                  pl.BlockSpec((D_MODEL, tn), lambda i, j: (0, j)),
                  pl.BlockSpec((D_MODEL, SMALL_COLS), lambda i, j: (0, 0))],
        out_specs=[pl.BlockSpec((tm, tn), lambda i, j: (i, j)),
                   pl.BlockSpec((tm, SMALL_COLS), lambda i, j: (i, 0))],
        out_shape=[jax.ShapeDtypeStruct((m, U_COLS), BF16),
                   jax.ShapeDtypeStruct((m, SMALL_COLS), F32)],
        compiler_params=_params(("parallel", "arbitrary")),
        name="in_proj",
    )(h, w_main, w_small)


def _gla_constants():
    c = GLA_CHUNK
    mat = np.zeros((1 + GLA_MXU_LEVELS, c, c), np.float32)
    t = np.arange(c)[:, None]
    tau = np.arange(c)[None, :]
    mat[0] = tau <= t
    for lvl in range(GLA_MXU_LEVELS):
        s = 1 << lvl
        mid = (t // (2 * s)) * (2 * s) + s
        upper = t >= mid
        mat[1 + lvl] = np.where(upper, (tau >= mid) & (tau <= t), (tau > t) & (tau <= mid - 1))
    return jnp.asarray(mat.reshape((1 + GLA_MXU_LEVELS) * c, c), BF16)


def _gla_level_mask(lvl):
    c = GLA_CHUNK
    s = 1 << lvl
    i = lax.broadcasted_iota(jnp.int32, (c, c), 0)
    j = lax.broadcasted_iota(jnp.int32, (c, c), 1)
    same = (i // (2 * s)) == (j // (2 * s))
    return same & ((i % (2 * s)) >= s) & ((j % (2 * s)) < s)


def _gla_kernel(q_ref, k_ref, v_ref, s_ref, w2_ref, b2_ref, mat_ref, o_ref, st_ref, *, n_chunks):
    c = GLA_CHUNK

    @pl.when(pl.program_id(1) == 0)
    def _():
        st_ref[...] = jnp.zeros_like(st_ref)

    w2 = w2_ref[...]
    b2 = b2_ref[...]
    mat = mat_ref[...]
    ii = lax.broadcasted_iota(jnp.int32, (c, c), 0)
    jj = lax.broadcasted_iota(jnp.int32, (c, c), 1)
    masks = [_gla_level_mask(lvl) for lvl in range(GLA_LEVELS)]
    eye = ii == jj
    heads = range(GLA_HEADS)
    kcs = [slice(h * GLA_DK, (h + 1) * GLA_DK) for h in heads]
    vcs = [slice(h * GLA_DV, (h + 1) * GLA_DV) for h in heads]

    def ref_rows(b, s):
        blk = b.reshape(c // (2 * s), 2 * s, GLA_DK)
        return jnp.broadcast_to(blk[:, s - 1:s, :], blk.shape).reshape(c, GLA_DK)

    def chunk(ci, carry):
        r0 = pl.multiple_of(ci * c, c)
        rows = pl.ds(r0, c)
        l_hi, l_lo = _split_hi_lo(s_ref[rows, :])
        logit = _dot(jnp.concatenate([l_hi, l_lo, l_hi], axis=1), w2) + b2
        g_all = _log_sigmoid(logit) * (LOG2E / GLA_TAU)

        x_inter, x_state, x_lvl, d_col = [], [], [], []
        for h in heads:
            e2 = _dot(mat, _hi_lo(g_all[:, kcs[h]]))
            e = e2[:, :GLA_DK] + e2[:, GLA_DK:]
            b = e[0:c]
            x_inter.append(jnp.exp2(b))
            x_state.append(jnp.exp2(b[c - 1:c] - b))
            lv = [jnp.exp2(e[(1 + l) * c:(2 + l) * c]) for l in range(GLA_MXU_LEVELS)]
            lv += [jnp.exp2(-jnp.abs(b - ref_rows(b, 1 << l)))
                   for l in range(GLA_MXU_LEVELS, GLA_LEVELS)]
            x_lvl.append(lv)
            d_col.append(jnp.exp2(b.T[:, c - 1:c]))

        amat = []
        for h in heads:
            q = q_ref[rows, kcs[h]].astype(F32) * (GLA_DK ** -0.5)
            k = k_ref[rows, kcs[h]].astype(F32)
            a = jnp.where(eye, _dot_nt(q.astype(BF16), k.astype(BF16)), 0.0)
            for lvl in range(GLA_LEVELS):
                xl = x_lvl[h][lvl]
                p = _dot_nt((q * xl).astype(BF16), (k * xl).astype(BF16))
                a = a + jnp.where(masks[lvl], p, 0.0)
            amat.append(a.astype(BF16))

        for h in heads:
            q = q_ref[rows, kcs[h]].astype(F32) * (GLA_DK ** -0.5)
            k = k_ref[rows, kcs[h]].astype(F32)
            v = v_ref[rows, vcs[h]]
            st = st_ref[h]
            o = _dot((q * x_inter[h]).astype(BF16), st.astype(BF16)) + _dot(amat[h], v)
            o_ref[rows, vcs[h]] = o.astype(BF16)
            ks = (k * x_state[h]).astype(BF16)
            st_ref[h] = st * d_col[h] + _dot_tn(ks, v)
        return carry

    lax.fori_loop(0, n_chunks, chunk, 0, unroll=4)


def _gla(u, small, w2, b2, mat, batch, seq, tb):
    m = batch * seq
    nt = seq // tb
    row = lambda b, t: b * nt + t
    kern = functools.partial(_gla_kernel, n_chunks=tb // GLA_CHUNK)
    return pl.pallas_call(
        kern,
        grid=(batch, nt),
        in_specs=[
            pl.BlockSpec((tb, GLA_KEY), lambda b, t: (row(b, t), OFF_Q // GLA_KEY)),
            pl.BlockSpec((tb, GLA_KEY), lambda b, t: (row(b, t), OFF_K // GLA_KEY)),
            pl.BlockSpec((tb, GLA_VAL), lambda b, t: (row(b, t), OFF_V // GLA_VAL)),
            pl.BlockSpec((tb, SMALL_COLS), lambda b, t: (row(b, t), 0)),
            pl.BlockSpec((3 * SMALL_COLS, GLA_KEY), lambda b, t: (0, 0)),
            pl.BlockSpec((1, GLA_KEY), lambda b, t: (0, 0)),
            pl.BlockSpec(((1 + GLA_MXU_LEVELS) * GLA_CHUNK, GLA_CHUNK), lambda b, t: (0, 0)),
        ],
        out_specs=pl.BlockSpec((tb, GLA_VAL), lambda b, t: (row(b, t), 0)),
        out_shape=jax.ShapeDtypeStruct((m, GLA_VAL), BF16),
        scratch_shapes=[pltpu.VMEM((GLA_HEADS, GLA_DK, GLA_DV), F32)],
        compiler_params=_params(("parallel", "arbitrary")),
        name="gla_mixer",
    )(u, u, u, small, w2, b2, mat)


def _ssd_constants():
    ex = np.zeros((SSM_GROUPS, SMALL_COLS, SSM_GW), np.float32)
    for g in range(SSM_GROUPS):
        for j in range(SSM_HPG):
            ex[g, DT_LANE0 + g * SSM_HPG + j, j * SSM_P:(j + 1) * SSM_P] = 1.0
    ln = SSM_CHUNK
    tril = np.tril(np.ones((ln, ln), np.float32))
    tri2 = np.concatenate([tril, 1.0 - tril], axis=0)
    shift = np.zeros((SSM_CONV - 1, ln, ln + HIST_ROWS), np.float32)
    for k in range(SSM_CONV - 1):
        shift[k, np.arange(ln), HIST_ROWS + np.arange(ln) - (SSM_CONV - 1) + k] = 1.0
    return (jnp.asarray(ex, BF16), jnp.asarray(tri2, BF16),
            jnp.asarray(shift.reshape((SSM_CONV - 1) * ln, ln + HIST_ROWS), BF16))


def _ssd_kernel(x_ref, bm_ref, cm_ref, xh_ref, bh_ref, ch_ref, s_ref, cw_ref, cb_ref,
                dtb_ref, arow_ref, dexp_ref, ex_ref, tri2_ref, shift_ref,
                o_ref, st_ref, *, n_chunks, tb):
    ln = SSM_CHUNK
    hr = HIST_ROWS
    gw = SSM_GW
    first = pl.program_id(2) == 0

    @pl.when(first)
    def _():
        st_ref[...] = jnp.zeros_like(st_ref)

    def raw_rows(lo, hi):
        return jnp.concatenate([x_ref[lo:hi, :], bm_ref[lo:hi, :], cm_ref[lo:hi, :]], axis=1)

    hist = jnp.concatenate([xh_ref[...], bh_ref[...], ch_ref[...]], axis=1)
    hist = jnp.where(first, jnp.zeros_like(hist), hist)

    tri2 = tri2_ref[...]
    shift = shift_ref[...]
    exm = ex_ref[...]
    cw = cw_ref[...]
    cbias = cb_ref[...]
    dtb = dtb_ref[...]
    a_row = arow_ref[...]
    d_exp = dexp_ref[...]
    ii = lax.broadcasted_iota(jnp.int32, (ln, ln), 0)
    jj = lax.broadcasted_iota(jnp.int32, (ln, ln), 1)
    causal = jj <= ii
    low_half = lax.broadcasted_iota(jnp.int32, (ln, 2 * SSM_P), 1) < SSM_P
    loc_shift = SMALL_COLS - DT_LANE0 - pl.program_id(1) * SSM_HPG

    def prep(ci):
        r0 = ci * ln
        if ci == 0:
            ext = jnp.concatenate([hist, raw_rows(0, ln)], axis=0)
        else:
            ext = raw_rows(r0 - hr, r0 + ln)
        taps = _dot(shift, ext)
        acc = cbias + ext[hr:, :].astype(F32) * cw[SSM_CONV - 1:SSM_CONV, :]
        for k in range(SSM_CONV - 1):
            acc = acc + taps[k * ln:(k + 1) * ln] * cw[k:k + 1, :]
        xbc = _silu_half(acc)
        xc = xbc[:, :gw]
        bc = xbc[:, gw:gw + SSM_N].astype(BF16)
        cc = xbc[:, gw + SSM_N:].astype(BF16)

        dt_small = _softplus(s_ref[r0:r0 + ln, :] + dtb)
        cs = _dot(tri2, _hi_lo(dt_small * a_row))
        cum_small = cs[:ln, :SMALL_COLS] + cs[:ln, SMALL_COLS:]
        rcum_small = cs[ln:, :SMALL_COLS] + cs[ln:, SMALL_COLS:]
        small3 = jnp.concatenate([dt_small.astype(BF16), jnp.exp2(cum_small).astype(BF16),
                                  jnp.exp2(rcum_small).astype(BF16)], axis=0)
        wide3 = _dot(small3, exm)
        cum_loc = pltpu.roll(cum_small, loc_shift, axis=1)
        cum_t = cum_loc.T
        return xc, bc, cc, wide3[:ln], wide3[ln:2 * ln], wide3[2 * ln:], cum_loc, cum_t

    def finish(ci, front):
        r0 = ci * ln
        xc, bc, cc, dt_exp, ecum, edst, cum_loc, cum_t = front
        xdt = xc * dt_exp
        xdt_b = xdt.astype(BF16)
        st = st_ref[...]
        y_off = _dot(cc, st.astype(BF16)) * ecum
        cbm = jnp.where(causal, _dot_nt(cc, bc), 0.0)
        yd = []
        for p in range(SSM_HPG // 2):
            ws, xm = [], []
            xp = xdt_b[:, 2 * p * SSM_P:(2 * p + 2) * SSM_P]
            for j, keep in ((2 * p, low_half), (2 * p + 1, ~low_half)):
                seg = cum_loc[:, j:j + 1] - cum_t[j:j + 1, :]
                ws.append((cbm * jnp.exp2(jnp.minimum(seg, 0.0))).astype(BF16))
                xm.append(jnp.where(keep, xp, jnp.zeros_like(xp)))
            yd.append(_dot(jnp.concatenate(ws, axis=1), jnp.concatenate(xm, axis=0)))
        y = y_off + jnp.concatenate(yd, axis=1)

        xd = (xdt * edst).astype(BF16)
        st_ref[...] = st * ecum[ln - 1:ln, :] + _dot_tn(bc, xd)

        o_ref[r0:r0 + ln, :] = (y + xc * d_exp).astype(BF16)

    front = prep(0)
    for ci in range(n_chunks):
        nxt = prep(ci + 1) if ci + 1 < n_chunks else None
        finish(ci, front)
        front = nxt


def _ssd(u, small, cw, cb, dtb_row, a_row, d_exp, exm, tri2, shift, batch, seq, tb):
    m = batch * seq
    nt = seq // tb
    row = lambda b, g, t: b * nt + t
    gcol = lambda off, width: (lambda b, g, t: (row(b, g, t), off // width + g))
    hist_row = lambda b, g, t: jnp.maximum(row(b, g, t) * (tb // HIST_ROWS) - 1, 0)
    hcol = lambda off, width: (lambda b, g, t: (hist_row(b, g, t), off // width + g))
    const2 = lambda shape: pl.BlockSpec(shape, lambda b, g, t: (0, 0))
    per_group = lambda r, c: pl.BlockSpec((None, r, c), lambda b, g, t: (g, 0, 0))
    kern = functools.partial(_ssd_kernel, n_chunks=tb // SSM_CHUNK, tb=tb)
    return pl.pallas_call(
        kern,
        grid=(batch, SSM_GROUPS, nt),
        in_specs=[
            pl.BlockSpec((tb, SSM_GW), gcol(OFF_X, SSM_GW)),
            pl.BlockSpec((tb, SSM_N), gcol(OFF_B, SSM_N)),
            pl.BlockSpec((tb, SSM_N), gcol(OFF_C, SSM_N)),
            pl.BlockSpec((HIST_ROWS, SSM_GW), hcol(OFF_X, SSM_GW)),
            pl.BlockSpec((HIST_ROWS, SSM_N), hcol(OFF_B, SSM_N)),
            pl.BlockSpec((HIST_ROWS, SSM_N), hcol(OFF_C, SSM_N)),
            pl.BlockSpec((tb, SMALL_COLS), lambda b, g, t: (row(b, g, t), 0)),
            per_group(SSM_CONV, SSM_CW),
            per_group(1, SSM_CW),
            const2((1, SMALL_COLS)),
            const2((1, SMALL_COLS)),
            pl.BlockSpec((1, SSM_GW), lambda b, g, t: (0, g)),
            per_group(SMALL_COLS, SSM_GW),
            const2((2 * SSM_CHUNK, SSM_CHUNK)),
            const2(((SSM_CONV - 1) * SSM_CHUNK, SSM_CHUNK + HIST_ROWS)),
        ],
        out_specs=pl.BlockSpec((tb, SSM_GW), lambda b, g, t: (row(b, g, t), g)),
        out_shape=jax.ShapeDtypeStruct((m, SSM_INNER), BF16),
        scratch_shapes=[pltpu.VMEM((SSM_N, SSM_GW), F32)],
        compiler_params=_params(("parallel", "parallel", "arbitrary")),
        name="ssd_mixer",
    )(u, u, u, u, u, u, small, cw, cb, dtb_row, a_row, d_exp, exm, tri2, shift)


def _group_rms(x, width):
    parts = []
    for c0 in range(0, x.shape[1], width):
        xg = x[:, c0:c0 + width]
        parts.append(xg * lax.rsqrt(jnp.mean(xg * xg, axis=-1, keepdims=True) + EPS))
    return jnp.concatenate(parts, axis=1)


def _merge_kernel(oa_ref, r_ref, ob_ref, z_ref, ga_ref, gb_ref, x_ref,
                  wa_ref, wb_ref, wo_ref, nw_ref, xo_ref, h_ref):
    oa = _group_rms(oa_ref[...].astype(F32), GLA_DV) * _silu_half(r_ref[...].astype(F32))
    ya = _dot(oa.astype(BF16), wa_ref[...])
    ob = ob_ref[...].astype(F32) * _silu_half(z_ref[...].astype(F32))
    yb = _dot(_group_rms(ob, SSM_GW).astype(BF16), wb_ref[...])
    mix = (_sigmoid2_half(ga_ref[...].astype(F32)) * ya
           + _sigmoid2_half(gb_ref[...].astype(F32)) * yb)
    xn = x_ref[...] + _dot(mix.astype(BF16), wo_ref[...])
    xo_ref[...] = xn
    h_ref[...] = _rms_rows(xn, nw_ref[...]).astype(BF16)


def _merge(oa, ob, u, x, wa, wb, wo, nw, tm):
    m = x.shape[0]
    full = lambda shape: pl.BlockSpec(shape, lambda i: (0, 0))
    return pl.pallas_call(
        _merge_kernel,
        grid=(m // tm,),
        in_specs=[pl.BlockSpec((tm, GLA_VAL), lambda i: (i, 0)),
                  pl.BlockSpec((tm, GLA_VAL), lambda i: (i, OFF_R // GLA_VAL)),
                  pl.BlockSpec((tm, SSM_INNER), lambda i: (i, 0)),
                  pl.BlockSpec((tm, SSM_INNER), lambda i: (i, OFF_Z // SSM_INNER)),
                  pl.BlockSpec((tm, D_MODEL), lambda i: (i, OFF_GA // D_MODEL)),
                  pl.BlockSpec((tm, D_MODEL), lambda i: (i, OFF_GB // D_MODEL)),
                  pl.BlockSpec((tm, D_MODEL), lambda i: (i, 0)),
                  full((GLA_VAL, D_MODEL)), full((SSM_INNER, D_MODEL)),
                  full((D_MODEL, D_MODEL)), full((1, D_MODEL))],
        out_specs=[pl.BlockSpec((tm, D_MODEL), lambda i: (i, 0)),
                   pl.BlockSpec((tm, D_MODEL), lambda i: (i, 0))],
        out_shape=[jax.ShapeDtypeStruct((m, D_MODEL), F32),
                   jax.ShapeDtypeStruct((m, D_MODEL), BF16)],
        compiler_params=_params(("parallel",)),
        name="merge_out",
    )(oa, u, ob, u, u, u, x, wa, wb, wo, nw)


def _ffn_kernel(h_ref, x_ref, wg_ref, wu_ref, wo_ref, nw_ref, o_ref, *, last):
    h = h_ref[...]
    gate = _dot(h, wg_ref[...])
    up = _dot(h, wu_ref[...])
    act = (_silu_half(gate) * up).astype(BF16)
    xn = x_ref[...] + _dot(act, wo_ref[...])
    if last:
        o_ref[...] = _rms_rows(xn, nw_ref[...])
    else:
        o_ref[0][...] = xn
        o_ref[1][...] = _rms_rows(xn, nw_ref[...]).astype(BF16)


def _ffn_kernel_mid(h_ref, x_ref, wg_ref, wu_ref, wo_ref, nw_ref, xo_ref, ho_ref):
    _ffn_kernel(h_ref, x_ref, wg_ref, wu_ref, wo_ref, nw_ref, (xo_ref, ho_ref), last=False)


def _ffn_kernel_last(h_ref, x_ref, wg_ref, wu_ref, wo_ref, nw_ref, o_ref):
    _ffn_kernel(h_ref, x_ref, wg_ref, wu_ref, wo_ref, nw_ref, o_ref, last=True)


def _ffn(h, x, wgu, wo, nw, tm, last):
    m = x.shape[0]
    once = pl.Buffered(1)
    full = lambda shape, col=0: pl.BlockSpec(shape, lambda i: (0, col), pipeline_mode=once)
    rows = pl.BlockSpec((tm, D_MODEL), lambda i: (i, 0))
    if last:
        kern, out_specs = _ffn_kernel_last, rows
        out_shape = jax.ShapeDtypeStruct((m, D_MODEL), F32)
    else:
        kern, out_specs = _ffn_kernel_mid, [rows, rows]
        out_shape = [jax.ShapeDtypeStruct((m, D_MODEL), F32),
                     jax.ShapeDtypeStruct((m, D_MODEL), BF16)]
    return pl.pallas_call(
        kern,
        grid=(m // tm,),
        in_specs=[rows, rows, full((D_MODEL, FFN_HIDDEN)), full((D_MODEL, FFN_HIDDEN), 1),
                  full((FFN_HIDDEN, D_MODEL)), full((1, D_MODEL))],
        out_specs=out_specs,
        out_shape=out_shape,
        compiler_params=_params(("parallel",)),
        name="ffn_last" if last else "ffn",
    )(h, x, wgu, wgu, wo, nw)


def _tiles(batch, seq):
    m = batch * seq
    pick = lambda n, cands: next(c for c in cands if n % c == 0)
    return dict(
        norm=pick(m, (1024, 512, 256, 128)),
        inproj_m=pick(m, (1024, 512, 256, 128)),
        inproj_n=2560,
        gla=pick(seq, (1024, 512, 256, 128, 64)),
        ssd=pick(seq, (2048, 1024, 512, 256, 128)),
        merge=pick(m, (512, 256, 128)),
        ffn=pick(m, (512, 256, 128)),
    )


def _ssd_group_cols(a):
    parts = []
    for g in range(SSM_GROUPS):
        b0 = SSM_INNER + g * SSM_N
        c0 = SSM_INNER + SSM_GROUPS * SSM_N + g * SSM_N
        parts.append(jnp.concatenate(
            [a[..., g * SSM_GW:(g + 1) * SSM_GW], a[..., b0:b0 + SSM_N], a[..., c0:c0 + SSM_N]],
            axis=-1))
    return jnp.stack(parts, axis=1)


def kernel(x, norm1_w, w_in, gla_gate_w2, gla_gate_b, gla_norm_w, ssm_conv_w, ssm_conv_b,
           ssm_dt_bias, ssm_A_log, ssm_D, ssm_norm_w, w_branch_a, w_branch_b, w_mix_out,
           norm2_w, w_ffn_in, w_ffn_out, final_norm_w):
    batch, seq, _ = x.shape
    depth = w_in.shape[0]
    m = batch * seq
    tl = _tiles(batch, seq)
    gla_mat = _gla_constants()
    ssd_ex, ssd_tri2, ssd_shift = _ssd_constants()

    offs = np.cumsum((0,) + IN_SPLITS)
    o_r, o_glr, o_z, o_xbc, o_dt, o_ga = (int(offs[i]) for i in (3, 4, 5, 6, 7, 8))
    IN_DIM = int(offs[-1])

    def small_rows(v):
        return jnp.pad(v, ((0, 0), (DT_LANE0, SMALL_COLS - DT_LANE0 - SSM_HEADS)))[:, None, :]

    w2_hi, w2_lo = _split_hi_lo(
        jnp.pad(gla_gate_w2, ((0, 0), (0, SMALL_COLS - GLA_RANK), (0, 0))))
    w2_all = jnp.concatenate([w2_hi, w2_hi, w2_lo], axis=1)
    conv_w_all = _ssd_group_cols(0.5 * ssm_conv_w)
    conv_b_all = _ssd_group_cols(0.5 * ssm_conv_b[:, None, :])
    dt_bias_all = small_rows(ssm_dt_bias)
    a_all = small_rows(-LOG2E * jnp.exp(ssm_A_log))
    d_all = jnp.repeat(ssm_D, SSM_P, axis=1)[:, None, :]

    gate_half = np.ones((1, IN_DIM), np.float32)
    gate_half[:, o_r:o_glr] = 0.5
    gate_half[:, o_z:o_xbc] = 0.5
    gate_half[:, o_ga:] = 0.5

    w_in_b = (w_in * gate_half).astype(BF16)
    wa_b = (w_branch_a * jnp.tile(gla_norm_w, (1, GLA_HEADS))[:, :, None]).astype(BF16)
    wb_b = (w_branch_b * ssm_norm_w[:, :, None]).astype(BF16)
    wmix_b = (0.5 * w_mix_out).astype(BF16)
    ffn_half = np.ones((1, 2 * FFN_HIDDEN), np.float32)
    ffn_half[:, :FFN_HIDDEN] = 0.5
    wffn_in_b = (w_ffn_in * ffn_half).astype(BF16)
    wffn_out_b = w_ffn_out.astype(BF16)

    xf = x.reshape(m, D_MODEL)
    h = _norm(xf, norm1_w[0].reshape(1, D_MODEL), tl["norm"])
    for l in range(depth):
        w = w_in_b[l]
        w_main = jnp.concatenate(
            [w[:, :o_r], w[:, o_z:o_xbc], w[:, o_r:o_glr], w[:, o_xbc:o_dt], w[:, o_ga:]], axis=1)
        w_small = jnp.concatenate(
            [w[:, o_glr:o_z], w[:, o_dt:o_ga],
             jnp.zeros((D_MODEL, SMALL_COLS - GLA_RANK - SSM_HEADS), BF16)], axis=1)
        u, small = _inproj(h, w_main, w_small, tl["inproj_m"], tl["inproj_n"])

        oa = _gla(u, small, w2_all[l], gla_gate_b[l].reshape(1, GLA_KEY), gla_mat,
                  batch, seq, tl["gla"])
        ob = _ssd(u, small, conv_w_all[l], conv_b_all[l], dt_bias_all[l], a_all[l], d_all[l],
                  ssd_ex, ssd_tri2, ssd_shift, batch, seq, tl["ssd"])
        xf, h2 = _merge(oa, ob, u, xf, wa_b[l], wb_b[l], wmix_b[l],
                        norm2_w[l].reshape(1, D_MODEL), tl["merge"])

        wgu, wo = wffn_in_b[l], wffn_out_b[l]
        if l + 1 < depth:
            xf, h = _ffn(h2, xf, wgu, wo, norm1_w[l + 1].reshape(1, D_MODEL), tl["ffn"], False)
        else:
            xf = _ffn(h2, xf, wgu, wo, final_norm_w.reshape(1, D_MODEL), tl["ffn"], True)
    return xf.reshape(batch, seq, D_MODEL)
```

```python
import functools

import numpy as np
import jax
import jax.numpy as jnp
from jax import lax
from jax.experimental import pallas as pl
from jax.experimental.pallas import tpu as pltpu

F32 = jnp.float32
BF16 = jnp.bfloat16

D_MODEL = 1024
EPS = 1e-6
GLA_HEADS = 4
GLA_DK = 128
GLA_DV = 256
GLA_KEY = GLA_HEADS * GLA_DK
GLA_VAL = GLA_HEADS * GLA_DV
GLA_RANK = 16
GLA_TAU = 16.0
GLA_CHUNK = 64
GLA_LEVELS = 6
GLA_MXU_LEVELS = 2
SSM_INNER = 2048
SSM_P = 64
SSM_HEADS = 32
SSM_GROUPS = 4
SSM_N = 128
SSM_CONV = 4
SSM_CHUNK = 128
SSM_HPG = SSM_HEADS // SSM_GROUPS
SSM_GW = SSM_HPG * SSM_P
SSM_CW = SSM_GW + 2 * SSM_N
SSM_XBC = SSM_INNER + 2 * SSM_GROUPS * SSM_N
FFN_HIDDEN = 2816
IN_SPLITS = (GLA_KEY, GLA_KEY, GLA_VAL, GLA_VAL, GLA_RANK, SSM_INNER, SSM_XBC,
             SSM_HEADS, D_MODEL, D_MODEL)

OFF_Q = 0
OFF_K = OFF_Q + GLA_KEY
OFF_V = OFF_K + GLA_KEY
OFF_Z = OFF_V + GLA_VAL
OFF_R = OFF_Z + SSM_INNER
OFF_X = OFF_R + GLA_VAL
OFF_B = OFF_X + SSM_INNER
OFF_C = OFF_B + SSM_GROUPS * SSM_N
OFF_GA = OFF_C + SSM_GROUPS * SSM_N
OFF_GB = OFF_GA + D_MODEL
U_COLS = OFF_GB + D_MODEL
SMALL_COLS = 128
DT_LANE0 = GLA_RANK
HIST_ROWS = 16

LOG2E = 1.4426950408889634

VMEM_LIMIT = 48 * 1024 * 1024


def _silu_half(hx):
    return hx * jnp.tanh(hx) + hx


def _sigmoid2_half(hx):
    return jnp.tanh(hx) + 1.0


def _softplus(x):
    return jnp.maximum(x, 0.0) + jnp.log1p(jnp.exp(-jnp.abs(x)))


def _log_sigmoid(x):
    return jnp.minimum(x, 0.0) - jnp.log1p(jnp.exp(-jnp.abs(x)))


def _split_hi_lo(x):
    hi = x.astype(BF16)
    lo = (x - hi.astype(F32)).astype(BF16)
    return hi, lo


def _hi_lo(x):
    hi, lo = _split_hi_lo(x)
    return jnp.concatenate([hi, lo], axis=1)


def _dot(a, b):
    return jnp.dot(a, b, preferred_element_type=F32)


def _dot_nt(a, b):
    return lax.dot_general(a, b, (((1,), (1,)), ((), ())), preferred_element_type=F32)


def _dot_tn(a, b):
    return lax.dot_general(a, b, (((0,), (0,)), ((), ())), preferred_element_type=F32)


def _params(semantics):
    return pltpu.CompilerParams(dimension_semantics=semantics, vmem_limit_bytes=VMEM_LIMIT)


def _rms_rows(x, w):
    return x * lax.rsqrt(jnp.mean(x * x, axis=-1, keepdims=True) + EPS) * w


def _norm_kernel(x_ref, w_ref, h_ref):
    h_ref[...] = _rms_rows(x_ref[...], w_ref[...]).astype(BF16)


def _norm(x, w, tm):
    m = x.shape[0]
    return pl.pallas_call(
        _norm_kernel,
        grid=(m // tm,),
        in_specs=[pl.BlockSpec((tm, D_MODEL), lambda i: (i, 0)),
                  pl.BlockSpec((1, D_MODEL), lambda i: (0, 0))],
        out_specs=pl.BlockSpec((tm, D_MODEL), lambda i: (i, 0)),
        out_shape=jax.ShapeDtypeStruct((m, D_MODEL), BF16),
        compiler_params=_params(("parallel",)),
        name="rms_norm",
    )(x, w)


def _inproj_kernel(h_ref, w_ref, ws_ref, u_ref, s_ref):
    h = h_ref[...]
    u_ref[...] = _dot(h, w_ref[...]).astype(BF16)

    @pl.when(pl.program_id(1) == 0)
    def _():
        s_ref[...] = _dot(h, ws_ref[...])


def _inproj(h, w_main, w_small, tm, tn):
    m = h.shape[0]
    return pl.pallas_call(
        _inproj_kernel,
        grid=(m // tm, U_COLS // tn),
        in_specs=[pl.BlockSpec((tm, D_MODEL), lambda i, j: (i, 0)),
                  pl.BlockSpec((D_MODEL, tn), lambda i, j: (0, j)),
                  pl.BlockSpec((D_MODEL, SMALL_COLS), lambda i, j: (0, 0))],
        out_specs=[pl.BlockSpec((tm, tn), lambda i, j: (i, j)),
                   pl.BlockSpec((tm, SMALL_COLS), lambda i, j: (i, 0))],
        out_shape=[jax.ShapeDtypeStruct((m, U_COLS), BF16),
                   jax.ShapeDtypeStruct((m, SMALL_COLS), F32)],
        compiler_params=_params(("parallel", "arbitrary")),
        name="in_proj",
    )(h, w_main, w_small)


def _gla_constants():
    c = GLA_CHUNK
    mat = np.zeros((1 + GLA_MXU_LEVELS, c, c), np.float32)
    t = np.arange(c)[:, None]
    tau = np.arange(c)[None, :]
    mat[0] = tau <= t
    for lvl in range(GLA_MXU_LEVELS):
        s = 1 << lvl
        mid = (t // (2 * s)) * (2 * s) + s
        upper = t >= mid
        mat[1 + lvl] = np.where(upper, (tau >= mid) & (tau <= t), (tau > t) & (tau <= mid - 1))
    return jnp.asarray(mat.reshape((1 + GLA_MXU_LEVELS) * c, c), BF16)


def _gla_level_mask(lvl):
    c = GLA_CHUNK
    s = 1 << lvl
    i = lax.broadcasted_iota(jnp.int32, (c, c), 0)
    j = lax.broadcasted_iota(jnp.int32, (c, c), 1)
    same = (i // (2 * s)) == (j // (2 * s))
    return same & ((i % (2 * s)) >= s) & ((j % (2 * s)) < s)


def _gla_kernel(q_ref, k_ref, v_ref, s_ref, w2_ref, b2_ref, mat_ref, o_ref, st_ref, *, n_chunks):
    c = GLA_CHUNK

    @pl.when(pl.program_id(1) == 0)
    def _():
        st_ref[...] = jnp.zeros_like(st_ref)

    w2 = w2_ref[...]
    b2 = b2_ref[...]
    mat = mat_ref[...]
    ii = lax.broadcasted_iota(jnp.int32, (c, c), 0)
    jj = lax.broadcasted_iota(jnp.int32, (c, c), 1)
    masks = [_gla_level_mask(lvl) for lvl in range(GLA_LEVELS)]
    eye = ii == jj
    heads = range(GLA_HEADS)
    kcs = [slice(h * GLA_DK, (h + 1) * GLA_DK) for h in heads]
    vcs = [slice(h * GLA_DV, (h + 1) * GLA_DV) for h in heads]

    def ref_rows(b, s):
        blk = b.reshape(c // (2 * s), 2 * s, GLA_DK)
        return jnp.broadcast_to(blk[:, s - 1:s, :], blk.shape).reshape(c, GLA_DK)

    def chunk(ci, carry):
        r0 = pl.multiple_of(ci * c, c)
        rows = pl.ds(r0, c)
        l_hi, l_lo = _split_hi_lo(s_ref[rows, :])
        logit = _dot(jnp.concatenate([l_hi, l_lo, l_hi], axis=1), w2) + b2
        g_all = _log_sigmoid(logit) * (LOG2E / GLA_TAU)

        x_inter, x_state, x_lvl, d_col = [], [], [], []
        for h in heads:
            e2 = _dot(mat, _hi_lo(g_all[:, kcs[h]]))
            e = e2[:, :GLA_DK] + e2[:, GLA_DK:]
            b = e[0:c]
            x_inter.append(jnp.exp2(b))
            x_state.append(jnp.exp2(b[c - 1:c] - b))
            lv = [jnp.exp2(e[(1 + l) * c:(2 + l) * c]) for l in range(GLA_MXU_LEVELS)]
            lv += [jnp.exp2(-jnp.abs(b - ref_rows(b, 1 << l)))
                   for l in range(GLA_MXU_LEVELS, GLA_LEVELS)]
            x_lvl.append(lv)
            d_col.append(jnp.exp2(b.T[:, c - 1:c]))

        amat = []
        for h in heads:
            q = q_ref[rows, kcs[h]].astype(F32) * (GLA_DK ** -0.5)
            k = k_ref[rows, kcs[h]].astype(F32)
            a = jnp.where(eye, _dot_nt(q.astype(BF16), k.astype(BF16)), 0.0)
            for lvl in range(GLA_LEVELS):
                xl = x_lvl[h][lvl]
                p = _dot_nt((q * xl).astype(BF16), (k * xl).astype(BF16))
                a = a + jnp.where(masks[lvl], p, 0.0)
            amat.append(a.astype(BF16))

        for h in heads:
            q = q_ref[rows, kcs[h]].astype(F32) * (GLA_DK ** -0.5)
            k = k_ref[rows, kcs[h]].astype(F32)
            v = v_ref[rows, vcs[h]]
            st = st_ref[h]
            o = _dot((q * x_inter[h]).astype(BF16), st.astype(BF16)) + _dot(amat[h], v)
            o_ref[rows, vcs[h]] = o.astype(BF16)
            ks = (k * x_state[h]).astype(BF16)
            st_ref[h] = st * d_col[h] + _dot_tn(ks, v)
        return carry

    lax.fori_loop(0, n_chunks, chunk, 0, unroll=16)


def _gla(u, small, w2, b2, mat, batch, seq, tb):
    m = batch * seq
    nt = seq // tb
    row = lambda b, t: b * nt + t
    kern = functools.partial(_gla_kernel, n_chunks=tb // GLA_CHUNK)
    return pl.pallas_call(
        kern,
        grid=(batch, nt),
        in_specs=[
            pl.BlockSpec((tb, GLA_KEY), lambda b, t: (row(b, t), OFF_Q // GLA_KEY)),
            pl.BlockSpec((tb, GLA_KEY), lambda b, t: (row(b, t), OFF_K // GLA_KEY)),
            pl.BlockSpec((tb, GLA_VAL), lambda b, t: (row(b, t), OFF_V // GLA_VAL)),
            pl.BlockSpec((tb, SMALL_COLS), lambda b, t: (row(b, t), 0)),
            pl.BlockSpec((3 * SMALL_COLS, GLA_KEY), lambda b, t: (0, 0)),
            pl.BlockSpec((1, GLA_KEY), lambda b, t: (0, 0)),
            pl.BlockSpec(((1 + GLA_MXU_LEVELS) * GLA_CHUNK, GLA_CHUNK), lambda b, t: (0, 0)),
        ],
        out_specs=pl.BlockSpec((tb, GLA_VAL), lambda b, t: (row(b, t), 0)),
        out_shape=jax.ShapeDtypeStruct((m, GLA_VAL), BF16),
        scratch_shapes=[pltpu.VMEM((GLA_HEADS, GLA_DK, GLA_DV), F32)],
        compiler_params=_params(("parallel", "arbitrary")),
        name="gla_mixer",
    )(u, u, u, small, w2, b2, mat)


def _ssd_constants():
    ex = np.zeros((SSM_GROUPS, SMALL_COLS, SSM_GW), np.float32)
    for g in range(SSM_GROUPS):
        for j in range(SSM_HPG):
            ex[g, DT_LANE0 + g * SSM_HPG + j, j * SSM_P:(j + 1) * SSM_P] = 1.0
    ln = SSM_CHUNK
    tril = np.tril(np.ones((ln, ln), np.float32))
    tri2 = np.concatenate([tril, 1.0 - tril], axis=0)
    shift = np.zeros((SSM_CONV - 1, ln, ln + HIST_ROWS), np.float32)
    for k in range(SSM_CONV - 1):
        shift[k, np.arange(ln), HIST_ROWS + np.arange(ln) - (SSM_CONV - 1) + k] = 1.0
    return (jnp.asarray(ex, BF16), jnp.asarray(tri2, BF16),
            jnp.asarray(shift.reshape((SSM_CONV - 1) * ln, ln + HIST_ROWS), BF16))


def _ssd_kernel(x_ref, bm_ref, cm_ref, xh_ref, bh_ref, ch_ref, s_ref, cw_ref, cb_ref,
                dtb_ref, arow_ref, dexp_ref, ex_ref, tri2_ref, shift_ref,
                o_ref, st_ref, *, n_chunks, tb):
    ln = SSM_CHUNK
    hr = HIST_ROWS
    gw = SSM_GW
    first = pl.program_id(2) == 0

    @pl.when(first)
    def _():
        st_ref[...] = jnp.zeros_like(st_ref)

    def raw_rows(lo, hi):
        return jnp.concatenate([x_ref[lo:hi, :], bm_ref[lo:hi, :], cm_ref[lo:hi, :]], axis=1)

    hist = jnp.concatenate([xh_ref[...], bh_ref[...], ch_ref[...]], axis=1)
    hist = jnp.where(first, jnp.zeros_like(hist), hist)

    tri2 = tri2_ref[...]
    shift = shift_ref[...]
    exm = ex_ref[...]
    cw = cw_ref[...]
    cbias = cb_ref[...]
    dtb = dtb_ref[...]
    a_row = arow_ref[...]
    d_exp = dexp_ref[...]
    ii = lax.broadcasted_iota(jnp.int32, (ln, ln), 0)
    jj = lax.broadcasted_iota(jnp.int32, (ln, ln), 1)
    causal = jj <= ii
    low_half = lax.broadcasted_iota(jnp.int32, (ln, 2 * SSM_P), 1) < SSM_P
    loc_shift = SMALL_COLS - DT_LANE0 - pl.program_id(1) * SSM_HPG

    def prep(ci):
        r0 = ci * ln
        if ci == 0:
            ext = jnp.concatenate([hist, raw_rows(0, ln)], axis=0)
        else:
            ext = raw_rows(r0 - hr, r0 + ln)
        taps = _dot(shift, ext)
        acc = cbias + ext[hr:, :].astype(F32) * cw[SSM_CONV - 1:SSM_CONV, :]
        for k in range(SSM_CONV - 1):
            acc = acc + taps[k * ln:(k + 1) * ln] * cw[k:k + 1, :]
        xbc = _silu_half(acc)
        xc = xbc[:, :gw]
        bc = xbc[:, gw:gw + SSM_N].astype(BF16)
        cc = xbc[:, gw + SSM_N:].astype(BF16)

        dt_small = _softplus(s_ref[r0:r0 + ln, :] + dtb)
        cs = _dot(tri2, _hi_lo(dt_small * a_row))
        cum_small = cs[:ln, :SMALL_COLS] + cs[:ln, SMALL_COLS:]
        rcum_small = cs[ln:, :SMALL_COLS] + cs[ln:, SMALL_COLS:]
        small3 = jnp.concatenate([dt_small.astype(BF16), jnp.exp2(cum_small).astype(BF16),
                                  jnp.exp2(rcum_small).astype(BF16)], axis=0)
        wide3 = _dot(small3, exm)
        cum_loc = pltpu.roll(cum_small, loc_shift, axis=1)
        cum_t = cum_loc.T
        return xc, bc, cc, wide3[:ln], wide3[ln:2 * ln], wide3[2 * ln:], cum_loc, cum_t

    def finish(ci, front):
        r0 = ci * ln
        xc, bc, cc, dt_exp, ecum, edst, cum_loc, cum_t = front
        xdt = xc * dt_exp
        xdt_b = xdt.astype(BF16)
        st = st_ref[...]
        y_off = _dot(cc, st.astype(BF16)) * ecum
        cbm = jnp.where(causal, _dot_nt(cc, bc), 0.0)
        yd = []
        for p in range(SSM_HPG // 2):
            ws, xm = [], []
            xp = xdt_b[:, 2 * p * SSM_P:(2 * p + 2) * SSM_P]
            for j, keep in ((2 * p, low_half), (2 * p + 1, ~low_half)):
                seg = cum_loc[:, j:j + 1] - cum_t[j:j + 1, :]
                ws.append((cbm * jnp.exp2(jnp.minimum(seg, 0.0))).astype(BF16))
                xm.append(jnp.where(keep, xp, jnp.zeros_like(xp)))
            yd.append(_dot(jnp.concatenate(ws, axis=1), jnp.concatenate(xm, axis=0)))
        y = y_off + jnp.concatenate(yd, axis=1)

        xd = (xdt * edst).astype(BF16)
        st_ref[...] = st * ecum[ln - 1:ln, :] + _dot_tn(bc, xd)

        o_ref[r0:r0 + ln, :] = (y + xc * d_exp).astype(BF16)

    front = prep(0)
    for ci in range(n_chunks):
        nxt = prep(ci + 1) if ci + 1 < n_chunks else None
        finish(ci, front)
        front = nxt


def _ssd(u, small, cw, cb, dtb_row, a_row, d_exp, exm, tri2, shift, batch, seq, tb):
    m = batch * seq
    nt = seq // tb
    row = lambda b, g, t: b * nt + t
    gcol = lambda off, width: (lambda b, g, t: (row(b, g, t), off // width + g))
    hist_row = lambda b, g, t: jnp.maximum(row(b, g, t) * (tb // HIST_ROWS) - 1, 0)
    hcol = lambda off, width: (lambda b, g, t: (hist_row(b, g, t), off // width + g))
    const2 = lambda shape: pl.BlockSpec(shape, lambda b, g, t: (0, 0))
    per_group = lambda r, c: pl.BlockSpec((None, r, c), lambda b, g, t: (g, 0, 0))
    kern = functools.partial(_ssd_kernel, n_chunks=tb // SSM_CHUNK, tb=tb)
    return pl.pallas_call(
        kern,
        grid=(batch, SSM_GROUPS, nt),
        in_specs=[
            pl.BlockSpec((tb, SSM_GW), gcol(OFF_X, SSM_GW)),
            pl.BlockSpec((tb, SSM_N), gcol(OFF_B, SSM_N)),
            pl.BlockSpec((tb, SSM_N), gcol(OFF_C, SSM_N)),
            pl.BlockSpec((HIST_ROWS, SSM_GW), hcol(OFF_X, SSM_GW)),
            pl.BlockSpec((HIST_ROWS, SSM_N), hcol(OFF_B, SSM_N)),
            pl.BlockSpec((HIST_ROWS, SSM_N), hcol(OFF_C, SSM_N)),
            pl.BlockSpec((tb, SMALL_COLS), lambda b, g, t: (row(b, g, t), 0)),
            per_group(SSM_CONV, SSM_CW),
            per_group(1, SSM_CW),
            const2((1, SMALL_COLS)),
            const2((1, SMALL_COLS)),
            pl.BlockSpec((1, SSM_GW), lambda b, g, t: (0, g)),
            per_group(SMALL_COLS, SSM_GW),
            const2((2 * SSM_CHUNK, SSM_CHUNK)),
            const2(((SSM_CONV - 1) * SSM_CHUNK, SSM_CHUNK + HIST_ROWS)),
        ],
        out_specs=pl.BlockSpec((tb, SSM_GW), lambda b, g, t: (row(b, g, t), g)),
        out_shape=jax.ShapeDtypeStruct((m, SSM_INNER), BF16),
        scratch_shapes=[pltpu.VMEM((SSM_N, SSM_GW), F32)],
        compiler_params=_params(("parallel", "parallel", "arbitrary")),
        name="ssd_mixer",
    )(u, u, u, u, u, u, small, cw, cb, dtb_row, a_row, d_exp, exm, tri2, shift)


def _group_rms(x, width):
    parts = []
    for c0 in range(0, x.shape[1], width):
        xg = x[:, c0:c0 + width]
        parts.append(xg * lax.rsqrt(jnp.mean(xg * xg, axis=-1, keepdims=True) + EPS))
    return jnp.concatenate(parts, axis=1)


def _merge_kernel(oa_ref, r_ref, ob_ref, z_ref, ga_ref, gb_ref, x_ref,
                  wa_ref, wb_ref, wo_ref, nw_ref, xo_ref, h_ref):
    oa = _group_rms(oa_ref[...].astype(F32), GLA_DV) * _silu_half(r_ref[...].astype(F32))
    ya = _dot(oa.astype(BF16), wa_ref[...])
    ob = ob_ref[...].astype(F32) * _silu_half(z_ref[...].astype(F32))
    yb = _dot(_group_rms(ob, SSM_GW).astype(BF16), wb_ref[...])
    mix = (_sigmoid2_half(ga_ref[...].astype(F32)) * ya
           + _sigmoid2_half(gb_ref[...].astype(F32)) * yb)
    xn = x_ref[...] + _dot(mix.astype(BF16), wo_ref[...])
    xo_ref[...] = xn
    h_ref[...] = _rms_rows(xn, nw_ref[...]).astype(BF16)


def _merge(oa, ob, u, x, wa, wb, wo, nw, tm):
    m = x.shape[0]
    full = lambda shape: pl.BlockSpec(shape, lambda i: (0, 0))
    return pl.pallas_call(
        _merge_kernel,
        grid=(m // tm,),
        in_specs=[pl.BlockSpec((tm, GLA_VAL), lambda i: (i, 0)),
                  pl.BlockSpec((tm, GLA_VAL), lambda i: (i, OFF_R // GLA_VAL)),
                  pl.BlockSpec((tm, SSM_INNER), lambda i: (i, 0)),
                  pl.BlockSpec((tm, SSM_INNER), lambda i: (i, OFF_Z // SSM_INNER)),
                  pl.BlockSpec((tm, D_MODEL), lambda i: (i, OFF_GA // D_MODEL)),
                  pl.BlockSpec((tm, D_MODEL), lambda i: (i, OFF_GB // D_MODEL)),
                  pl.BlockSpec((tm, D_MODEL), lambda i: (i, 0)),
                  full((GLA_VAL, D_MODEL)), full((SSM_INNER, D_MODEL)),
                  full((D_MODEL, D_MODEL)), full((1, D_MODEL))],
        out_specs=[pl.BlockSpec((tm, D_MODEL), lambda i: (i, 0)),
                   pl.BlockSpec((tm, D_MODEL), lambda i: (i, 0))],
        out_shape=[jax.ShapeDtypeStruct((m, D_MODEL), F32),
                   jax.ShapeDtypeStruct((m, D_MODEL), BF16)],
        compiler_params=_params(("parallel",)),
        name="merge_out",
    )(oa, u, ob, u, u, u, x, wa, wb, wo, nw)


def _ffn_kernel(h_ref, x_ref, wg_ref, wu_ref, wo_ref, nw_ref, o_ref, *, last):
    h = h_ref[...]
    gate = _dot(h, wg_ref[...])
    up = _dot(h, wu_ref[...])
    act = (_silu_half(gate) * up).astype(BF16)
    xn = x_ref[...] + _dot(act, wo_ref[...])
    if last:
        o_ref[...] = _rms_rows(xn, nw_ref[...])
    else:
        o_ref[0][...] = xn
        o_ref[1][...] = _rms_rows(xn, nw_ref[...]).astype(BF16)


def _ffn_kernel_mid(h_ref, x_ref, wg_ref, wu_ref, wo_ref, nw_ref, xo_ref, ho_ref):
    _ffn_kernel(h_ref, x_ref, wg_ref, wu_ref, wo_ref, nw_ref, (xo_ref, ho_ref), last=False)


def _ffn_kernel_last(h_ref, x_ref, wg_ref, wu_ref, wo_ref, nw_ref, o_ref):
    _ffn_kernel(h_ref, x_ref, wg_ref, wu_ref, wo_ref, nw_ref, o_ref, last=True)


def _ffn(h, x, wgu, wo, nw, tm, last):
    m = x.shape[0]
    once = pl.Buffered(1)
    full = lambda shape, col=0: pl.BlockSpec(shape, lambda i: (0, col), pipeline_mode=once)
    rows = pl.BlockSpec((tm, D_MODEL), lambda i: (i, 0))
    if last:
        kern, out_specs = _ffn_kernel_last, rows
        out_shape = jax.ShapeDtypeStruct((m, D_MODEL), F32)
    else:
        kern, out_specs = _ffn_kernel_mid, [rows, rows]
        out_shape = [jax.ShapeDtypeStruct((m, D_MODEL), F32),
                     jax.ShapeDtypeStruct((m, D_MODEL), BF16)]
    return pl.pallas_call(
        kern,
        grid=(m // tm,),
        in_specs=[rows, rows, full((D_MODEL, FFN_HIDDEN)), full((D_MODEL, FFN_HIDDEN), 1),
                  full((FFN_HIDDEN, D_MODEL)), full((1, D_MODEL))],
        out_specs=out_specs,
        out_shape=out_shape,
        compiler_params=_params(("parallel",)),
        name="ffn_last" if last else "ffn",
    )(h, x, wgu, wgu, wo, nw)


def _tiles(batch, seq):
    m = batch * seq
    pick = lambda n, cands: next(c for c in cands if n % c == 0)
    return dict(
        norm=pick(m, (1024, 512, 256, 128)),
        inproj_m=pick(m, (1024, 512, 256, 128)),
        inproj_n=2560,
        gla=pick(seq, (1024, 512, 256, 128, 64)),
        ssd=pick(seq, (2048, 1024, 512, 256, 128)),
        merge=pick(m, (512, 256, 128)),
        ffn=pick(m, (512, 256, 128)),
    )


def _ssd_group_cols(a):
    parts = []
    for g in range(SSM_GROUPS):
        b0 = SSM_INNER + g * SSM_N
        c0 = SSM_INNER + SSM_GROUPS * SSM_N + g * SSM_N
        parts.append(jnp.concatenate(
            [a[..., g * SSM_GW:(g + 1) * SSM_GW], a[..., b0:b0 + SSM_N], a[..., c0:c0 + SSM_N]],
            axis=-1))
    return jnp.stack(parts, axis=1)


def kernel(x, norm1_w, w_in, gla_gate_w2, gla_gate_b, gla_norm_w, ssm_conv_w, ssm_conv_b,
           ssm_dt_bias, ssm_A_log, ssm_D, ssm_norm_w, w_branch_a, w_branch_b, w_mix_out,
           norm2_w, w_ffn_in, w_ffn_out, final_norm_w):
    batch, seq, _ = x.shape
    depth = w_in.shape[0]
    m = batch * seq
    tl = _tiles(batch, seq)
    gla_mat = _gla_constants()
    ssd_ex, ssd_tri2, ssd_shift = _ssd_constants()

    offs = np.cumsum((0,) + IN_SPLITS)
    o_r, o_glr, o_z, o_xbc, o_dt, o_ga = (int(offs[i]) for i in (3, 4, 5, 6, 7, 8))
    IN_DIM = int(offs[-1])

    def small_rows(v):
        return jnp.pad(v, ((0, 0), (DT_LANE0, SMALL_COLS - DT_LANE0 - SSM_HEADS)))[:, None, :]

    w2_hi, w2_lo = _split_hi_lo(
        jnp.pad(gla_gate_w2, ((0, 0), (0, SMALL_COLS - GLA_RANK), (0, 0))))
    w2_all = jnp.concatenate([w2_hi, w2_hi, w2_lo], axis=1)
    conv_w_all = _ssd_group_cols(0.5 * ssm_conv_w)
    conv_b_all = _ssd_group_cols(0.5 * ssm_conv_b[:, None, :])
    dt_bias_all = small_rows(ssm_dt_bias)
    a_all = small_rows(-LOG2E * jnp.exp(ssm_A_log))
    d_all = jnp.repeat(ssm_D, SSM_P, axis=1)[:, None, :]

    gate_half = np.ones((1, IN_DIM), np.float32)
    gate_half[:, o_r:o_glr] = 0.5
    gate_half[:, o_z:o_xbc] = 0.5
    gate_half[:, o_ga:] = 0.5

    w_in_b = (w_in * gate_half).astype(BF16)
    wa_b = (w_branch_a * jnp.tile(gla_norm_w, (1, GLA_HEADS))[:, :, None]).astype(BF16)
    wb_b = (w_branch_b * ssm_norm_w[:, :, None]).astype(BF16)
    wmix_b = (0.5 * w_mix_out).astype(BF16)
    ffn_half = np.ones((1, 2 * FFN_HIDDEN), np.float32)
    ffn_half[:, :FFN_HIDDEN] = 0.5
    wffn_in_b = (w_ffn_in * ffn_half).astype(BF16)
    wffn_out_b = w_ffn_out.astype(BF16)

    xf = x.reshape(m, D_MODEL)
    h = _norm(xf, norm1_w[0].reshape(1, D_MODEL), tl["norm"])
    for l in range(depth):
        w = w_in_b[l]
        w_main = jnp.concatenate(
            [w[:, :o_r], w[:, o_z:o_xbc], w[:, o_r:o_glr], w[:, o_xbc:o_dt], w[:, o_ga:]], axis=1)
        w_small = jnp.concatenate(
            [w[:, o_glr:o_z], w[:, o_dt:o_ga],
             jnp.zeros((D_MODEL, SMALL_COLS - GLA_RANK - SSM_HEADS), BF16)], axis=1)
        u, small = _inproj(h, w_main, w_small, tl["inproj_m"], tl["inproj_n"])

        oa = _gla(u, small, w2_all[l], gla_gate_b[l].reshape(1, GLA_KEY), gla_mat,
                  batch, seq, tl["gla"])
        ob = _ssd(u, small, conv_w_all[l], conv_b_all[l], dt_bias_all[l], a_all[l], d_all[l],
                  ssd_ex, ssd_tri2, ssd_shift, batch, seq, tl["ssd"])
        xf, h2 = _merge(oa, ob, u, xf, wa_b[l], wb_b[l], wmix_b[l],
                        norm2_w[l].reshape(1, D_MODEL), tl["merge"])

        wgu, wo = wffn_in_b[l], wffn_out_b[l]
        if l + 1 < depth:
            xf, h = _ffn(h2, xf, wgu, wo, norm1_w[l + 1].reshape(1, D_MODEL), tl["ffn"], False)
        else:
            xf = _ffn(h2, xf, wgu, wo, final_norm_w.reshape(1, D_MODEL), tl["ffn"], True)
    return xf.reshape(batch, seq, D_MODEL)
```
